```python
import jax, jax.numpy as jnp
from jax import lax
import numpy as np

D_MODEL = 2048
BATCH = 4
SEQ = 4096
DEPTH = 2

CONV_WIDTH = D_MODEL // 2
CONV_KERNEL = 3
ATTN_HEAD_DIM = 128
ATTN_HEADS_PER_GROUP = D_MODEL // 512
ATTN_PATTERNS = ((128, 1), (512, 4), (2048, 16))
ATTN_HEADS = ATTN_HEADS_PER_GROUP * len(ATTN_PATTERNS)
ATTN_WIDTH = ATTN_HEADS * ATTN_HEAD_DIM
ATTN_OUT_WIDTH = ATTN_HEADS_PER_GROUP * ATTN_HEAD_DIM
ATTN_BLOCK = 128
ROPE_THETA = 500000.0
ROPE_DIM = ATTN_HEAD_DIM // 4
POOL_WINDOWS = (2, 4, 8, 16)
POOL_GROUPS = len(POOL_WINDOWS)
POOL_WIDTH = D_MODEL // 2
POOL_GROUP_DIM = POOL_WIDTH // POOL_GROUPS
N_BRANCHES = 3
D_FF = 7 * D_MODEL // 2
N_EXPERTS = 8
TOP_K = 2
EXPERT_BLOCK = 256
LN_EPS = 1e-5
DEEPNORM_ALPHA = (2.0 * DEPTH) ** 0.25
DEEPNORM_BETA = (8.0 * DEPTH) ** -0.25
NEG_INF = -1e30

_W_IN_PARTS = (CONV_WIDTH,) * 3 + (ATTN_WIDTH,) * 3 + (POOL_WIDTH,) + (N_BRANCHES * D_MODEL,)
W_IN_COLS = sum(_W_IN_PARTS)
W_IN_SPLITS = tuple(sum(_W_IN_PARTS[:i + 1]) for i in range(len(_W_IN_PARTS) - 1))

kernel_name = 'hybrid_conv_dilated_attn_pool_moe_deepnorm'


def layer_norm(x, g, b):
    xf = x.astype(jnp.float32)
    mu = jnp.mean(xf, axis=-1, keepdims=True)
    var = jnp.mean(jnp.square(xf - mu), axis=-1, keepdims=True)
    return ((xf - mu) * lax.rsqrt(var + LN_EPS)).astype(x.dtype) * g + b


def rotary_tables(seq, dtype):
    inv_freq = ROPE_THETA ** (-jnp.arange(0, ROPE_DIM, 2, dtype=jnp.float32) / ROPE_DIM)
    ang = jnp.arange(seq, dtype=jnp.float32)[:, None] * inv_freq[None, :]
    return jnp.cos(ang).astype(dtype), jnp.sin(ang).astype(dtype)


def partial_rotary(t, cos, sin):
    half = ROPE_DIM // 2
    r1, r2, rest = t[..., :half], t[..., half:ROPE_DIM], t[..., ROPE_DIM:]
    c = cos[None, :, None, :]
    s = sin[None, :, None, :]
    return jnp.concatenate([r1 * c - r2 * s, r2 * c + r1 * s, rest], axis=-1)


def dilated_window_attention(q, k, v, dilation, span):
    b, s, h, dh = q.shape
    L = s // dilation
    qb_len = min(ATTN_BLOCK, L)
    nb = -(-L // qb_len)
    lp = nb * qb_len
    n_keys = qb_len + span

    def to_sub(t):
        return t.reshape(b, L, dilation, h, dh).transpose(0, 2, 1, 3, 4)

    qs = jnp.pad(to_sub(q), ((0, 0), (0, 0), (0, lp - L), (0, 0), (0, 0)))
    kv_pad = ((0, 0), (0, 0), (span, lp - L), (0, 0), (0, 0))
    ks = jnp.pad(to_sub(k), kv_pad)
    vs = jnp.pad(to_sub(v), kv_pad)
    blk = jnp.arange(nb)
    idx = blk[:, None] * qb_len + jnp.arange(n_keys)[None, :]
    kb = ks[:, :, idx]
    vb = vs[:, :, idx]
    qb = qs.reshape(b, dilation, nb, qb_len, h, dh)
    scores = jnp.einsum('brnqhd,brnkhd->brnhqk', qb, kb).astype(jnp.float32) * (dh ** -0.5)
    a = jnp.arange(qb_len)[None, :, None]
    kk = jnp.arange(n_keys)[None, None, :]
    valid = (kk >= a) & (kk <= a + span) & (blk[:, None, None] * qb_len + kk >= span)
    scores = jnp.where(valid[None, None, :, None], scores, NEG_INF)
    lse = jax.nn.logsumexp(scores, axis=-1)
    probs = jnp.exp(scores - lse[..., None]).astype(v.dtype)
    out = jnp.einsum('brnhqk,brnkhd->brnqhd', probs, vb)
    out = out.reshape(b, dilation, lp, h, dh)[:, :, :L].transpose(0, 2, 1, 3, 4).reshape(b, s, h, dh)
    lse = lse.transpose(0, 1, 2, 4, 3).reshape(b, dilation, lp, h)[:, :, :L]
    lse = lse.transpose(0, 2, 1, 3).reshape(b, s, h)
    return out, lse


def multiscale_pool(u, pool_mix, pool_scale):
    b, s, _ = u.shape
    ug = u.reshape(b, s, POOL_GROUPS, POOL_GROUP_DIM)
    cs = jnp.cumsum(ug.astype(jnp.float32), axis=1)
    t = jnp.arange(1, s + 1, dtype=jnp.float32)
    means = []
    for gi, w in enumerate(POOL_WINDOWS):
        c = cs[:, :, gi]
        prev = jnp.pad(c, ((0, 0), (w, 0), (0, 0)))[:, :s]
        means.append((c - prev) / jnp.minimum(t, float(w))[None, :, None])
    pooled = jnp.stack(means, axis=2).astype(u.dtype) - ug
    mixed = jnp.einsum('bsgc,gce->bsge', pooled, pool_mix).reshape(b, s, POOL_WIDTH)
    return mixed * pool_scale


def hybrid_mixer(x, w_in, conv_w, w_br_conv, w_br_attn, pool_mix, pool_scale, w_br_pool, w_o, cos, sin):
    b, s, d = x.shape
    proj = jnp.einsum('bsd,dn->bsn', x, w_in)
    c_b, c_c, c_h, q, k, v, p_u, g = jnp.split(proj, W_IN_SPLITS, axis=-1)
    u = c_c * c_h
    u_pad = jnp.pad(u, ((0, 0), (CONV_KERNEL - 1, 0), (0, 0)))
    conv = u_pad[:, 0:s] * conv_w[0]
    for j in range(1, CONV_KERNEL):
        conv = conv + u_pad[:, j:j + s] * conv_w[j]
    y_a = jnp.einsum('bsc,cd->bsd', c_b * conv, w_br_conv)
    q = partial_rotary(q.reshape(b, s, ATTN_HEADS, ATTN_HEAD_DIM), cos, sin)
    k = partial_rotary(k.reshape(b, s, ATTN_HEADS, ATTN_HEAD_DIM), cos, sin)
    v = v.reshape(b, s, ATTN_HEADS, ATTN_HEAD_DIM)
    outs, lses = [], []
    for gi, (window, dilation) in enumerate(ATTN_PATTERNS):
        hs = slice(gi * ATTN_HEADS_PER_GROUP, (gi + 1) * ATTN_HEADS_PER_GROUP)
        o, l = dilated_window_attention(q[:, :, hs], k[:, :, hs], v[:, :, hs], dilation, window // dilation)
        outs.append(o)
        lses.append(l)
    wts = jax.nn.softmax(jnp.stack(lses, axis=0), axis=0)
    attn = jnp.sum(wts[..., None] * jnp.stack(outs, axis=0).astype(jnp.float32), axis=0).astype(x.dtype)
    y_b = jnp.einsum('bsc,cd->bsd', attn.reshape(b, s, ATTN_OUT_WIDTH), w_br_attn)
    y_c = jnp.einsum('bsc,cd->bsd', multiscale_pool(p_u, pool_mix, pool_scale), w_br_pool)
    gates = jax.nn.sigmoid(g.astype(jnp.float32)).astype(x.dtype).reshape(b, s, N_BRANCHES, d)
    merged = gates[:, :, 0] * y_a + gates[:, :, 1] * y_b + gates[:, :, 2] * y_c
    return jnp.einsum('bsd,de->bse', merged, w_o)


def swiglu(x, w_gate, w_up, w_down):
    h = jax.nn.silu(jnp.einsum('bsd,df->bsf', x, w_gate)) * jnp.einsum('bsd,df->bsf', x, w_up)
    return jnp.einsum('bsf,fd->bsd', h, w_down)


def moe_swiglu(x, w_router, w_gate, w_up, w_down):
    b, s, d = x.shape
    n_tok = b * s
    n_assign = n_tok * TOP_K
    xf = x.reshape(n_tok, d)
    logits = jnp.einsum('nd,de->ne', xf, w_router).astype(jnp.float32)
    top_val, top_idx = lax.top_k(logits, TOP_K)
    top_gate = jax.nn.softmax(top_val, axis=-1)
    exp_flat = top_idx.reshape(-1)
    tok_flat = jnp.arange(n_assign, dtype=jnp.int32) // TOP_K
    gate_flat = top_gate.reshape(-1)
    order = jnp.argsort(exp_flat)
    exp_sorted = exp_flat[order]
    tok_sorted = tok_flat[order]
    gate_sorted = gate_flat[order]
    counts = jnp.zeros((N_EXPERTS,), jnp.int32).at[exp_flat].add(1)
    padded = (counts + EXPERT_BLOCK - 1) // EXPERT_BLOCK * EXPERT_BLOCK
    start = jnp.cumsum(counts) - counts
    pend = jnp.cumsum(padded)
    pstart = pend - padded
    rank = jnp.arange(n_assign, dtype=jnp.int32) - start[exp_sorted]
    dest = pstart[exp_sorted] + rank
    cap = -(-n_assign // EXPERT_BLOCK) * EXPERT_BLOCK + N_EXPERTS * EXPERT_BLOCK
    n_blk = cap // EXPERT_BLOCK
    buf_tok = jnp.zeros((cap,), jnp.int32).at[dest].set(tok_sorted)
    buf_gate = jnp.zeros((cap,), jnp.float32).at[dest].set(gate_sorted)
    blk_start = jnp.arange(n_blk, dtype=jnp.int32) * EXPERT_BLOCK
    blk_exp = jnp.clip(jnp.searchsorted(pend, blk_start, side='right'), 0, N_EXPERTS - 1)
    xs = xf[buf_tok].reshape(n_blk, EXPERT_BLOCK, d)

    def expert_block(args):
        xb, e = args
        h = jax.nn.silu(xb @ w_gate[e]) * (xb @ w_up[e])
        return h @ w_down[e]

    ys = lax.map(expert_block, (xs, blk_exp)).reshape(cap, d)
    ys = (ys.astype(jnp.float32) * buf_gate[:, None]).astype(x.dtype)
    out = jnp.zeros((n_tok, d), x.dtype).at[buf_tok].add(ys)
    return out.reshape(b, s, d)


def _normal(key, shape, scale):
    return jax.random.normal(key, shape, jnp.float32) * scale


def setup_inputs(seed: int = 0) -> dict:
    key = jax.random.key(seed)
    keys = jax.random.split(key, DEPTH + 1)
    params = {'x': jax.random.normal(keys[0], (BATCH, SEQ, D_MODEL), jnp.float32)}
    for layer in range(DEPTH):
        ks = jax.random.split(keys[layer + 1], 16)
        p = 'l%d_' % layer
        params[p + 'w_in'] = _normal(ks[0], (D_MODEL, W_IN_COLS), D_MODEL ** -0.5)
        params[p + 'conv_w'] = _normal(ks[1], (CONV_KERNEL, CONV_WIDTH), CONV_KERNEL ** -0.5)
        params[p + 'w_br_conv'] = _normal(ks[2], (CONV_WIDTH, D_MODEL), CONV_WIDTH ** -0.5)
        params[p + 'w_br_attn'] = _normal(ks[3], (ATTN_OUT_WIDTH, D_MODEL), ATTN_OUT_WIDTH ** -0.5)
        params[p + 'pool_mix'] = _normal(ks[4], (POOL_GROUPS, POOL_GROUP_DIM, POOL_GROUP_DIM), POOL_GROUP_DIM ** -0.5)
        params[p + 'pool_scale'] = 1.0 + _normal(ks[5], (POOL_WIDTH,), 0.1)
        params[p + 'w_br_pool'] = _normal(ks[6], (POOL_WIDTH, D_MODEL), POOL_WIDTH ** -0.5)
        params[p + 'w_o'] = _normal(ks[7], (D_MODEL, D_MODEL), D_MODEL ** -0.5 * DEEPNORM_BETA)
        params[p + 'ln1_g'] = 1.0 + _normal(ks[8], (D_MODEL,), 0.02)
        params[p + 'ln1_b'] = _normal(ks[9], (D_MODEL,), 0.02)
        if layer % 2 == 0:
            params[p + 'ffn_gate'] = _normal(ks[10], (D_MODEL, D_FF), D_MODEL ** -0.5)
            params[p + 'ffn_up'] = _normal(ks[11], (D_MODEL, D_FF), D_MODEL ** -0.5)
            params[p + 'ffn_down'] = _normal(ks[12], (D_FF, D_MODEL), D_FF ** -0.5 * DEEPNORM_BETA)
        else:
            params[p + 'router'] = _normal(ks[13], (D_MODEL, N_EXPERTS), D_MODEL ** -0.5)
            params[p + 'exp_gate'] = _normal(ks[10], (N_EXPERTS, D_MODEL, D_FF), D_MODEL ** -0.5)
            params[p + 'exp_up'] = _normal(ks[11], (N_EXPERTS, D_MODEL, D_FF), D_MODEL ** -0.5)
            params[p + 'exp_down'] = _normal(ks[12], (N_EXPERTS, D_FF, D_MODEL), D_FF ** -0.5 * DEEPNORM_BETA)
        params[p + 'ln2_g'] = 1.0 + _normal(ks[14], (D_MODEL,), 0.02)
        params[p + 'ln2_b'] = _normal(ks[15], (D_MODEL,), 0.02)
    return params


def reference(x,
              l0_w_in, l0_conv_w, l0_w_br_conv, l0_w_br_attn, l0_pool_mix, l0_pool_scale, l0_w_br_pool, l0_w_o,
              l0_ln1_g, l0_ln1_b, l0_ffn_gate, l0_ffn_up, l0_ffn_down, l0_ln2_g, l0_ln2_b,
              l1_w_in, l1_conv_w, l1_w_br_conv, l1_w_br_attn, l1_pool_mix, l1_pool_scale, l1_w_br_pool, l1_w_o,
              l1_ln1_g, l1_ln1_b, l1_router, l1_exp_gate, l1_exp_up, l1_exp_down, l1_ln2_g, l1_ln2_b):
    mixers = ((l0_w_in, l0_conv_w, l0_w_br_conv, l0_w_br_attn, l0_pool_mix, l0_pool_scale, l0_w_br_pool, l0_w_o),
              (l1_w_in, l1_conv_w, l1_w_br_conv, l1_w_br_attn, l1_pool_mix, l1_pool_scale, l1_w_br_pool, l1_w_o))
    norms = ((l0_ln1_g, l0_ln1_b, l0_ln2_g, l0_ln2_b), (l1_ln1_g, l1_ln1_b, l1_ln2_g, l1_ln2_b))
    ffns = ((l0_ffn_gate, l0_ffn_up, l0_ffn_down), (l1_router, l1_exp_gate, l1_exp_up, l1_exp_down))
    cos, sin = rotary_tables(x.shape[1], x.dtype)
    for layer in range(DEPTH):
        ln1_g, ln1_b, ln2_g, ln2_b = norms[layer]
        mix = hybrid_mixer(x, *mixers[layer], cos, sin)
        x = layer_norm(DEEPNORM_ALPHA * x + mix, ln1_g, ln1_b)
        if layer % 2 == 0:
            f = swiglu(x, *ffns[layer])
        else:
            f = moe_swiglu(x, *ffns[layer])
        x = layer_norm(DEEPNORM_ALPHA * x + f, ln2_g, ln2_b)
    return x
```

```python
import functools

import jax
import jax.numpy as jnp
from jax import lax
from jax.experimental import pallas as pl
from jax.experimental.pallas import tpu as pltpu

D_MODEL = 2048
SEQ = 4096
CONV_WIDTH = D_MODEL // 2
CONV_KERNEL = 3
HEAD_DIM = 128
HEADS_PER_GROUP = 4
GROUP_WIDTH = HEADS_PER_GROUP * HEAD_DIM
ATTN_PATTERNS = ((128, 1), (512, 4), (2048, 16))
N_GROUPS = len(ATTN_PATTERNS)
QKV_WIDTH = 3 * N_GROUPS * GROUP_WIDTH
ATTN_BLOCK = 128
ROPE_THETA = 500000.0
ROPE_DIM = HEAD_DIM // 4
POOL_WINDOWS = (2, 4, 8, 16)
POOL_WIDTH = D_MODEL // 2
POOL_GROUP_DIM = POOL_WIDTH // len(POOL_WINDOWS)
POOL_HALO = 16
CONV_HALO = 8
D_FF = 7 * D_MODEL // 2
N_EXPERTS = 8
TOP_K = 2
LN_EPS = 1e-5
DEPTH = 2
DEEPNORM_ALPHA = (2.0 * DEPTH) ** 0.25
NEG_INF = -1e30
LANES = 128

VMEM_LIMIT = 56 * 1024 * 1024


def _params(sem, vmem=VMEM_LIMIT):
    return pltpu.CompilerParams(dimension_semantics=sem, vmem_limit_bytes=vmem)


def _proj_kernel(x_ref, w_ref, o_ref, wbf_ref):
    @pl.when(pl.program_id(1) == 0)
    def _():
        wbf_ref[...] = w_ref[...].astype(jnp.bfloat16)

    o_ref[...] = jnp.dot(x_ref[...], wbf_ref[...],
                         preferred_element_type=jnp.float32).astype(o_ref.dtype)


def _project(x_bf, w, col0, width, tn, out_dtype, tm=1024):
    m, k = x_bf.shape
    assert col0 % tn == 0 and width % tn == 0 and m % tm == 0
    j0 = col0 // tn
    return pl.pallas_call(
        _proj_kernel,
        grid=(width // tn, m // tm),
        in_specs=[pl.BlockSpec((tm, k), lambda j, i: (i, 0)),
                  pl.BlockSpec((k, tn), lambda j, i: (0, j0 + j))],
        out_specs=pl.BlockSpec((tm, tn), lambda j, i: (i, j)),
        out_shape=jax.ShapeDtypeStruct((m, width), out_dtype),
        scratch_shapes=[pltpu.VMEM((k, tn), jnp.bfloat16)],
        compiler_params=_params(("arbitrary", "arbitrary")),
        name="in_proj",
    )(x_bf, w)


def _rope(t, cos, sin):
    half = ROPE_DIM // 2
    lane = lax.broadcasted_iota(jnp.int32, t.shape, 1)
    partner = jnp.where(lane < half, pltpu.roll(t, HEAD_DIM - half, 1), pltpu.roll(t, half, 1))
    return t * cos + partner * sin


def _attn_kernel(q_ref, k_ref, kh_ref, v_ref, vh_ref, cos_ref, sin_ref, cosh_ref, sinh_ref,
                 o_ref, lse_ref, qs_ref, ks_ref, vs_ref, *, lb):
    first_tile = pl.program_id(2) == 0
    cos, sin = cos_ref[...], sin_ref[...]
    cosh, sinh = cosh_ref[...], sinh_ref[...]
    for h in range(HEADS_PER_GROUP):
        cols = slice(h * HEAD_DIM, (h + 1) * HEAD_DIM)
        qs_ref[:, cols] = _rope(q_ref[0, :, cols].astype(jnp.float32), cos, sin).astype(jnp.bfloat16)
        ks_ref[ATTN_BLOCK:, cols] = _rope(k_ref[0, :, cols].astype(jnp.float32), cos, sin).astype(jnp.bfloat16)
        ks_ref[:ATTN_BLOCK, cols] = _rope(kh_ref[0, :, cols].astype(jnp.float32), cosh, sinh).astype(jnp.bfloat16)
    vs_ref[ATTN_BLOCK:, :] = v_ref[0]
    vs_ref[:ATTN_BLOCK, :] = vh_ref[0]

    n_keys = 2 * ATTN_BLOCK
    a = lax.broadcasted_iota(jnp.int32, (ATTN_BLOCK, n_keys), 0)
    kk = lax.broadcasted_iota(jnp.int32, (ATTN_BLOCK, n_keys), 1)
    band = (kk >= a) & (kk <= a + ATTN_BLOCK)
    scale = HEAD_DIM ** -0.5

    def block(blk, carry):
        r0 = pl.multiple_of(blk * ATTN_BLOCK, ATTN_BLOCK)
        first_key = jnp.where(first_tile & (blk == 0), ATTN_BLOCK, 0)
        valid = band & (kk >= first_key)
        for h in range(HEADS_PER_GROUP):
            cols = slice(h * HEAD_DIM, (h + 1) * HEAD_DIM)
            qb = qs_ref[pl.ds(r0, ATTN_BLOCK), cols]
            kb = ks_ref[pl.ds(r0, n_keys), cols]
            vb = vs_ref[pl.ds(r0, n_keys), cols]
            s = lax.dot_general(qb, kb, (((1,), (1,)), ((), ())),
                                preferred_element_type=jnp.float32) * scale
            s = jnp.where(valid, s, NEG_INF)
            m = jnp.max(s, axis=-1, keepdims=True)
            p = jnp.exp(s - m)
            l = jnp.sum(p, axis=-1, keepdims=True)
            o = jnp.dot(p.astype(jnp.bfloat16), vb, preferred_element_type=jnp.float32)
            o_ref[0, pl.ds(r0, ATTN_BLOCK), cols] = o / l
            lse_ref[0, 0, pl.ds(r0, ATTN_BLOCK), h:h + 1] = m + jnp.log(l)
        return carry

    lax.fori_loop(0, lb // ATTN_BLOCK, block, 0)


def _attention_group(qkv, rope_cos, rope_sin, group):
    b, s, _ = qkv.shape
    _, d = ATTN_PATTERNS[group]
    sub_len = s // d
    lb = min(sub_len, 1024)
    n_tiles = sub_len // lb
    halo_per_tile = lb // ATTN_BLOCK
    col_blocks = QKV_WIDTH // GROUP_WIDTH
    qkv_r = qkv.reshape(b, sub_len, d * QKV_WIDTH)
    cos_r = rope_cos.reshape(sub_len, d * LANES)
    sin_r = rope_sin.reshape(sub_len, d * LANES)

    def main(part):
        return pl.BlockSpec((1, lb, GROUP_WIDTH),
                            lambda bi, c, i: (bi, i, c * col_blocks + part * N_GROUPS + group))

    def halo(part):
        return pl.BlockSpec((1, ATTN_BLOCK, GROUP_WIDTH),
                            lambda bi, c, i: (bi, jnp.maximum(i * halo_per_tile - 1, 0),
                                              c * col_blocks + part * N_GROUPS + group))

    tab = pl.BlockSpec((lb, LANES), lambda bi, c, i: (i, c))
    tab_halo = pl.BlockSpec((ATTN_BLOCK, LANES),
                            lambda bi, c, i: (jnp.maximum(i * halo_per_tile - 1, 0), c))
    out, lse = pl.pallas_call(
        functools.partial(_attn_kernel, lb=lb),
        grid=(b, d, n_tiles),
        in_specs=[main(0), main(1), halo(1), main(2), halo(2), tab, tab, tab_halo, tab_halo],
        out_specs=[pl.BlockSpec((1, lb, GROUP_WIDTH), lambda bi, c, i: (bi, i, c)),
                   pl.BlockSpec((1, 1, lb, HEADS_PER_GROUP), lambda bi, c, i: (bi, c, i, 0))],
        out_shape=[jax.ShapeDtypeStruct((b, sub_len, d * GROUP_WIDTH), jnp.float32),
                   jax.ShapeDtypeStruct((b, d, sub_len, HEADS_PER_GROUP), jnp.float32)],
        scratch_shapes=[pltpu.VMEM((lb, GROUP_WIDTH), jnp.bfloat16),
                        pltpu.VMEM((lb + ATTN_BLOCK, GROUP_WIDTH), jnp.bfloat16),
                        pltpu.VMEM((lb + ATTN_BLOCK, GROUP_WIDTH), jnp.bfloat16)],
        compiler_params=_params(("arbitrary", "arbitrary", "arbitrary")),
        name="dilated_attn_g%d" % group,
    )(qkv_r, qkv_r, qkv_r, qkv_r, qkv_r, cos_r, sin_r, cos_r, sin_r)
    out = out.reshape(b * s, GROUP_WIDTH)
    lse = lse.transpose(0, 2, 1, 3).reshape(b * s, HEADS_PER_GROUP)
    return out, lse


def _branch_kernel(cb_ref, cc_ref, ch_ref, cch_ref, chh_ref, pu_ref, puh_ref,
                   a0_ref, a1_ref, a2_ref, l0_ref, l1_ref, l2_ref,
                   g0_ref, g1_ref, g2_ref, convw_ref, pscale_ref,
                   wconv_ref, wattn_ref, wpool_ref, pmix_ref, o_ref, *, tm, tiles_per_seq):
    seq_tile = pl.program_id(0) % tiles_per_seq
    not_first = (seq_tile != 0).astype(jnp.float32)

    u = cc_ref[...] * ch_ref[...]
    uh = cch_ref[...] * chh_ref[...] * not_first
    ext = jnp.concatenate([uh, u], axis=0)
    n_ext = tm + CONV_HALO
    u1 = pltpu.roll(ext, 1, 0)[CONV_HALO:]
    u2 = pltpu.roll(ext, 2, 0)[CONV_HALO:]
    cw = convw_ref[...]
    conv = u2 * cw[0:1] + u1 * cw[1:2]
    conv = conv + u * cw[2:3]
    ya_in = (cb_ref[...] * conv).astype(jnp.bfloat16)
    y_a = jnp.dot(ya_in, wconv_ref[...], preferred_element_type=jnp.float32)
    merged = jax.nn.sigmoid(g0_ref[...]) * y_a

    l0, l1, l2 = l0_ref[...], l1_ref[...], l2_ref[...]
    mx = jnp.maximum(jnp.maximum(l0, l1), l2)
    e0, e1, e2 = jnp.exp(l0 - mx), jnp.exp(l1 - mx), jnp.exp(l2 - mx)
    den = e0 + e1 + e2
    w0, w1, w2 = e0 / den, e1 / den, e2 / den
    heads = []
    for h in range(HEADS_PER_GROUP):
        cols = slice(h * HEAD_DIM, (h + 1) * HEAD_DIM)
        acc = w0[:, h:h + 1] * a0_ref[:, cols] + w1[:, h:h + 1] * a1_ref[:, cols]
        acc = acc + w2[:, h:h + 1] * a2_ref[:, cols]
        heads.append(acc.astype(jnp.bfloat16))
    attn = jnp.concatenate(heads, axis=1)
    y_b = jnp.dot(attn, wattn_ref[...], preferred_element_type=jnp.float32)
    merged = merged + jax.nn.sigmoid(g1_ref[...]) * y_b

    pu = pu_ref[...]
    pext = jnp.concatenate([puh_ref[...] * not_first, pu], axis=0)
    pos = seq_tile * tm + lax.broadcasted_iota(jnp.int32, (tm, 1), 0) + 1
    pscale = pscale_ref[...]
    y_c = None
    for gi, w in enumerate(POOL_WINDOWS):
        cols = slice(gi * POOL_GROUP_DIM, (gi + 1) * POOL_GROUP_DIM)
        run = pext[:, cols]
        step = 1
        while step < w:
            run = run + pltpu.roll(run, step, 0)
            step *= 2
        inv_cnt = 1.0 / jnp.minimum(pos, w).astype(jnp.float32)
        pooled = run[POOL_HALO:] * inv_cnt - pu[:, cols]
        mixed = jnp.dot(pooled.astype(jnp.bfloat16), pmix_ref[gi], preferred_element_type=jnp.float32)
        mixed = (mixed * pscale[:, cols]).astype(jnp.bfloat16)
        part = jnp.dot(mixed, wpool_ref[cols, :], preferred_element_type=jnp.float32)
        y_c = part if y_c is None else y_c + part
    merged = merged + jax.nn.sigmoid(g2_ref[...]) * y_c
    o_ref[...] = merged.astype(o_ref.dtype)


def _branches(conv_p, pool_p, attn_outs, attn_lses, gates_p, conv_w, pool_scale,
              w_conv_bf, w_attn_bf, w_pool_bf, pool_mix_bf, tm=256):
    m = conv_p.shape[0]
    tiles_per_seq = SEQ // tm
    row = lambda width, col: pl.BlockSpec((tm, width), lambda i: (i, col))

    def halo(rows, col):
        per_tile = tm // rows
        return pl.BlockSpec((rows, CONV_WIDTH), lambda i: (jnp.maximum(i * per_tile - 1, 0), col))

    full = lambda shape: pl.BlockSpec(shape, lambda i: (0,) * len(shape), pipeline_mode=pl.Buffered(1))
    return pl.pallas_call(
        functools.partial(_branch_kernel, tm=tm, tiles_per_seq=tiles_per_seq),
        grid=(m // tm,),
        in_specs=[row(CONV_WIDTH, 0), row(CONV_WIDTH, 1), row(CONV_WIDTH, 2),
                  halo(CONV_HALO, 1), halo(CONV_HALO, 2),
                  row(POOL_WIDTH, 0), halo(POOL_HALO, 0),
                  row(GROUP_WIDTH, 0), row(GROUP_WIDTH, 0), row(GROUP_WIDTH, 0),
                  row(HEADS_PER_GROUP, 0), row(HEADS_PER_GROUP, 0), row(HEADS_PER_GROUP, 0),
                  row(D_MODEL, 0), row(D_MODEL, 1), row(D_MODEL, 2),
                  full((CONV_KERNEL, CONV_WIDTH)), full((1, POOL_WIDTH)),
                  full((CONV_WIDTH, D_MODEL)), full((GROUP_WIDTH, D_MODEL)), full((POOL_WIDTH, D_MODEL)),
                  full((len(POOL_WINDOWS), POOL_GROUP_DIM, POOL_GROUP_DIM))],
        out_specs=pl.BlockSpec((tm, D_MODEL), lambda i: (i, 0)),
        out_shape=jax.ShapeDtypeStruct((m, D_MODEL), jnp.bfloat16),
        compiler_params=_params(("arbitrary",)),
        name="branches",
    )(conv_p, conv_p, conv_p, conv_p, conv_p, pool_p, pool_p,
      attn_outs[0], attn_outs[1], attn_outs[2], attn_lses[0], attn_lses[1], attn_lses[2],
      gates_p, gates_p, gates_p, conv_w, pool_scale.reshape(1, POOL_WIDTH),
      w_conv_bf, w_attn_bf, w_pool_bf, pool_mix_bf)


def _resid_layer_norm(x, f, g, b):
    z = DEEPNORM_ALPHA * x + f
    mu = jnp.mean(z, axis=-1, keepdims=True)
    zc = z - mu
    var = jnp.mean(zc * zc, axis=-1, keepdims=True)
    return zc * lax.rsqrt(var + LN_EPS) * g + b


def _out_proj_ln_kernel(m_ref, w_ref, x_ref, g_ref, b_ref, o_ref, obf_ref):
    f = jnp.dot(m_ref[...], w_ref[...], preferred_element_type=jnp.float32)
    y = _resid_layer_norm(x_ref[...], f, g_ref[...], b_ref[...])
    o_ref[...] = y
    obf_ref[...] = y.astype(jnp.bfloat16)


def _out_proj_ln(merged_bf, w_o_bf, x, g, b, tm=512):
    m = x.shape[0]
    row = pl.BlockSpec((tm, D_MODEL), lambda i: (i, 0))
    vec = pl.BlockSpec((1, D_MODEL), lambda i: (0, 0))
    return pl.pallas_call(
        _out_proj_ln_kernel,
        grid=(m // tm,),
        in_specs=[row, pl.BlockSpec((D_MODEL, D_MODEL), lambda i: (0, 0)), row, vec, vec],
        out_specs=[row, row],
        out_shape=[jax.ShapeDtypeStruct((m, D_MODEL), jnp.float32),
                   jax.ShapeDtypeStruct((m, D_MODEL), jnp.bfloat16)],
        compiler_params=_params(("arbitrary",)),
        name="out_proj_ln",
    )(merged_bf, w_o_bf, x, g.reshape(1, D_MODEL), b.reshape(1, D_MODEL))


def _ln_kernel(x_ref, f_ref, g_ref, b_ref, o_ref, obf_ref):
    y = _resid_layer_norm(x_ref[...], f_ref[...], g_ref[...], b_ref[...])
    o_ref[...] = y
    obf_ref[...] = y.astype(jnp.bfloat16)


def _resid_ln(x, f, g, b, tm=512):
    m = x.shape[0]
    row = pl.BlockSpec((tm, D_MODEL), lambda i: (i, 0))
    vec = pl.BlockSpec((1, D_MODEL), lambda i: (0, 0))
    return pl.pallas_call(
        _ln_kernel,
        grid=(m // tm,),
        in_specs=[row, row, vec, vec],
        out_specs=[row, row],
        out_shape=[jax.ShapeDtypeStruct((m, D_MODEL), jnp.float32),
                   jax.ShapeDtypeStruct((m, D_MODEL), jnp.bfloat16)],
        compiler_params=_params(("arbitrary",)),
        name="resid_ln",
    )(x, f, g.reshape(1, D_MODEL), b.reshape(1, D_MODEL))


def _swiglu_kernel(blk_exp_ref, n_used_ref, x_ref, wg_ref, wu_ref, wd_ref, o_ref, xbf_ref):
    i, j = pl.program_id(0), pl.program_id(1)
    used = i < n_used_ref[0]

    @pl.when(used & (j == 0))
    def _():
        xbf_ref[...] = x_ref[...].astype(jnp.bfloat16)

    @pl.when(used)
    def _():
        x = xbf_ref[...]
        gate = jnp.dot(x, wg_ref[0], preferred_element_type=jnp.float32)
        up = jnp.dot(x, wu_ref[0], preferred_element_type=jnp.float32)
        h = (gate * jax.nn.sigmoid(gate) * up).astype(jnp.bfloat16)
        part = jnp.dot(h, wd_ref[0], preferred_element_type=jnp.float32)

        @pl.when(j == 0)
        def _():
            o_ref[...] = part

        @pl.when(j != 0)
        def _():
            o_ref[...] += part

    @pl.when(jnp.logical_not(used) & (j == 0))
    def _():
        o_ref[...] = jnp.zeros_like(o_ref)


def _swiglu_blocks(x, w_gate_bf, w_up_bf, w_down_bf, blk_exp, n_used, tm, tf):
    r = x.shape[0]
    nf = D_FF // tf
    last = nf - 1

    def live(i, nu):
        return jnp.minimum(i, nu[0] - 1)

    def jj(i, j, nu):
        return jnp.where(i < nu[0], j, last)

    grid_spec = pltpu.PrefetchScalarGridSpec(
        num_scalar_prefetch=2,
        grid=(r // tm, nf),
        in_specs=[pl.BlockSpec((tm, D_MODEL), lambda i, j, be, nu: (live(i, nu), 0)),
                  pl.BlockSpec((1, D_MODEL, tf), lambda i, j, be, nu: (be[live(i, nu)], 0, jj(i, j, nu))),
                  pl.BlockSpec((1, D_MODEL, tf), lambda i, j, be, nu: (be[live(i, nu)], 0, jj(i, j, nu))),
                  pl.BlockSpec((1, tf, D_MODEL), lambda i, j, be, nu: (be[live(i, nu)], jj(i, j, nu), 0))],
        out_specs=pl.BlockSpec((tm, D_MODEL), lambda i, j, be, nu: (i, 0)),
        scratch_shapes=[pltpu.VMEM((tm, D_MODEL), jnp.bfloat16)],
    )
    return pl.pallas_call(
        _swiglu_kernel,
        grid_spec=grid_spec,
        out_shape=jax.ShapeDtypeStruct((r, D_MODEL), jnp.float32),
        compiler_params=_params(("arbitrary", "arbitrary")),
        name="swiglu_blocks",
    )(blk_exp, n_used, x, w_gate_bf, w_up_bf, w_down_bf)


def _router_kernel(x_ref, w_ref, idx_ref, gate_ref, rank_ref, cnt_ref, carry_ref, *, tm):
    @pl.when(pl.program_id(0) == 0)
    def _():
        carry_ref[...] = jnp.zeros_like(carry_ref)

    logits = jnp.dot(x_ref[...], w_ref[...], preferred_element_type=jnp.float32,
                     precision=lax.Precision.HIGHEST)
    lane = lax.broadcasted_iota(jnp.int32, logits.shape, 1)
    logits = jnp.where(lane < N_EXPERTS, logits, -jnp.inf)
    m1 = jnp.max(logits, axis=-1, keepdims=True)
    i1 = jnp.min(jnp.where(logits == m1, lane, LANES), axis=-1, keepdims=True)
    rest = jnp.where(lane == i1, -jnp.inf, logits)
    m2 = jnp.max(rest, axis=-1, keepdims=True)
    i2 = jnp.min(jnp.where(rest == m2, lane, LANES), axis=-1, keepdims=True)
    e = jnp.exp(m2 - m1)
    den = 1.0 + e
    idx_ref[:, 0:1] = i1
    idx_ref[:, 1:2] = i2
    gate_ref[:, 0:1] = 1.0 / den
    gate_ref[:, 1:2] = e / den

    oh1 = (lane == i1).astype(jnp.float32)
    oh2 = (lane == i2).astype(jnp.float32)
    both = oh1 + oh2
    r = lax.broadcasted_iota(jnp.int32, (tm, tm), 0)
    c = lax.broadcasted_iota(jnp.int32, (tm, tm), 1)
    strict_lower = (r > c).astype(jnp.bfloat16)
    before = jnp.dot(strict_lower, both.astype(jnp.bfloat16), preferred_element_type=jnp.float32)
    before = before + carry_ref[...]
    rank_ref[:, 0:1] = jnp.sum(oh1 * before, axis=-1, keepdims=True).astype(jnp.int32)
    rank_ref[:, 1:2] = jnp.sum(oh2 * before, axis=-1, keepdims=True).astype(jnp.int32)
    carry_ref[...] += jnp.sum(both, axis=0, keepdims=True)
    cnt_ref[...] = carry_ref[...]


def _route(x, w_router, tm=512):
    m = x.shape[0]
    w_pad = jnp.pad(w_router, ((0, 0), (0, LANES - N_EXPERTS)))
    pair = pl.BlockSpec((tm, TOP_K), lambda i: (i, 0))
    return pl.pallas_call(
        functools.partial(_router_kernel, tm=tm),
        grid=(m // tm,),
        in_specs=[pl.BlockSpec((tm, D_MODEL), lambda i: (i, 0)),
                  pl.BlockSpec((D_MODEL, LANES), lambda i: (0, 0))],
        out_specs=[pair, pair, pair, pl.BlockSpec((1, LANES), lambda i: (0, 0))],
        out_shape=[jax.ShapeDtypeStruct((m, TOP_K), jnp.int32),
                   jax.ShapeDtypeStruct((m, TOP_K), jnp.float32),
                   jax.ShapeDtypeStruct((m, TOP_K), jnp.int32),
                   jax.ShapeDtypeStruct((1, LANES), jnp.float32)],
        scratch_shapes=[pltpu.VMEM((1, LANES), jnp.float32)],
        compiler_params=_params(("arbitrary",)),
        name="router",
    )(x, w_pad)


def _dispatch_kernel(dest_ref, x_ref, xs_in_ref, xs_ref, sem, *, tm):
    del xs_in_ref

    def issue(t, carry):
        for k in range(TOP_K):
            pltpu.make_async_copy(x_ref.at[pl.ds(t, 1)],
                                  xs_ref.at[pl.ds(dest_ref[TOP_K * t + k], 1)], sem).start()
        return carry

    lax.fori_loop(0, tm, issue, 0)
    for k in range(TOP_K):
        pltpu.make_async_copy(x_ref, xs_ref.at[pl.ds(0, tm)], sem).wait()


def _dispatch(x, dest_flat, rows, tm=256):
    m = x.shape[0]
    zeros = jnp.zeros((rows, D_MODEL), x.dtype)
    return pl.pallas_call(
        functools.partial(_dispatch_kernel, tm=tm),
        grid=(m // tm,),
        in_specs=[pl.BlockSpec((TOP_K * tm,), lambda i: (i,), memory_space=pltpu.SMEM),
                  pl.BlockSpec((tm, D_MODEL), lambda i: (i, 0)),
                  pl.BlockSpec(memory_space=pl.ANY)],
        out_specs=pl.BlockSpec(memory_space=pl.ANY),
        out_shape=jax.ShapeDtypeStruct((rows, D_MODEL), x.dtype),
        scratch_shapes=[pltpu.SemaphoreType.DMA(())],
        input_output_aliases={2: 0},
        compiler_params=_params(("arbitrary",)),
        name="dispatch",
    )(dest_flat, x, zeros)


def _combine_ln_kernel(dest_ref, x_ref, gate_ref, g_ref, b_ref, y_ref, o_ref, ybuf_ref, sem, *, tm):
    def issue(t, carry):
        for k in range(TOP_K):
            pltpu.make_async_copy(y_ref.at[pl.ds(dest_ref[TOP_K * t + k], 1)],
                                  ybuf_ref.at[k, pl.ds(t, 1)], sem).start()
        return carry

    lax.fori_loop(0, tm, issue, 0)
    for k in range(TOP_K):
        pltpu.make_async_copy(y_ref.at[pl.ds(0, tm)], ybuf_ref.at[k], sem).wait()
    gate = gate_ref[...]
    f = ybuf_ref[0] * gate[:, 0:1] + ybuf_ref[1] * gate[:, 1:2]
    o_ref[...] = _resid_layer_norm(x_ref[...], f, g_ref[...], b_ref[...])


def _combine_ln(x, y, dest_flat, gates, g, b, tm=256):
    m = x.shape[0]
    row = pl.BlockSpec((tm, D_MODEL), lambda i: (i, 0))
    vec = pl.BlockSpec((1, D_MODEL), lambda i: (0, 0))
    return pl.pallas_call(
        functools.partial(_combine_ln_kernel, tm=tm),
        grid=(m // tm,),
        in_specs=[pl.BlockSpec((TOP_K * tm,), lambda i: (i,), memory_space=pltpu.SMEM),
                  row, pl.BlockSpec((tm, TOP_K), lambda i: (i, 0)), vec, vec,
                  pl.BlockSpec(memory_space=pl.ANY)],
        out_specs=row,
        out_shape=jax.ShapeDtypeStruct((m, D_MODEL), jnp.float32),
        scratch_shapes=[pltpu.VMEM((TOP_K, tm, D_MODEL), jnp.float32),
                        pltpu.SemaphoreType.DMA(())],
        compiler_params=_params(("arbitrary",)),
        name="combine_ln",
    )(dest_flat, x, gates, g.reshape(1, D_MODEL), b.reshape(1, D_MODEL), y)


def _rope_tables(seq):
    inv_freq = ROPE_THETA ** (-jnp.arange(0, ROPE_DIM, 2, dtype=jnp.float32) / ROPE_DIM)
    ang = jnp.arange(seq, dtype=jnp.float32)[:, None] * inv_freq[None, :]
    cos, sin = jnp.cos(ang), jnp.sin(ang)
    pad = HEAD_DIM - ROPE_DIM
    cos_full = jnp.concatenate([cos, cos, jnp.ones((seq, pad), jnp.float32)], axis=1)
    sin_full = jnp.concatenate([-sin, sin, jnp.zeros((seq, pad), jnp.float32)], axis=1)
    return cos_full, sin_full


def _mixer(x, x_bf, batch, w_in, conv_w, w_br_conv, w_br_attn, pool_mix, pool_scale, w_br_pool, w_o,
           ln_g, ln_b, rope_cos, rope_sin):
    conv_cols = CONV_KERNEL * CONV_WIDTH
    conv_p = _project(x_bf, w_in, 0, conv_cols, 1024, jnp.float32)
    qkv = _project(x_bf, w_in, conv_cols, QKV_WIDTH, 1536, jnp.bfloat16)
    pool_p = _project(x_bf, w_in, conv_cols + QKV_WIDTH, POOL_WIDTH, 512, jnp.float32)
    gates_p = _project(x_bf, w_in, conv_cols + QKV_WIDTH + POOL_WIDTH, N_GROUPS * D_MODEL, 512, jnp.float32)
    qkv = qkv.reshape(batch, SEQ, QKV_WIDTH)
    outs, lses = [], []
    for group in range(N_GROUPS):
        o, l = _attention_group(qkv, rope_cos, rope_sin, group)
        outs.append(o)
        lses.append(l)
    bf = lambda w: w.astype(jnp.bfloat16)
    merged = _branches(conv_p, pool_p, outs, lses, gates_p, conv_w, pool_scale,
                       bf(w_br_conv), bf(w_br_attn), bf(w_br_pool), bf(pool_mix))
    return _out_proj_ln(merged, bf(w_o), x, ln_g, ln_b)


def _dense_ffn(x, x_bf, w_gate, w_up, w_down, ln_g, ln_b, tm=1024, tf=512):
    m = x.shape[0]
    bf = lambda w: w.astype(jnp.bfloat16)[None]
    blk_exp = jnp.zeros((m // tm,), jnp.int32)
    n_used = jnp.full((1,), m // tm, jnp.int32)
    f = _swiglu_blocks(x_bf, bf(w_gate), bf(w_up), bf(w_down), blk_exp, n_used, tm, tf)
    return _resid_ln(x, f, ln_g, ln_b)


def _moe_ffn(x, w_router, w_gate, w_up, w_down, ln_g, ln_b, tm=512, tf=1024):
    m = x.shape[0]
    idx, gates, rank, cnt = _route(x, w_router)
    counts = cnt[0, :N_EXPERTS].astype(jnp.int32)
    padded = (counts + tm - 1) // tm * tm
    pend = jnp.cumsum(padded)
    pstart = pend - padded
    dest_flat = (pstart[idx] + rank).reshape(-1)
    n_blk = m * TOP_K // tm + N_EXPERTS
    blk_start = jnp.arange(n_blk, dtype=jnp.int32) * tm
    blk_exp = jnp.clip(jnp.searchsorted(pend, blk_start, side='right'), 0, N_EXPERTS - 1).astype(jnp.int32)
    n_used = (pend[-1:] // tm).astype(jnp.int32)
    xs = _dispatch(x, dest_flat, n_blk * tm)
    bf = lambda w: w.astype(jnp.bfloat16)
    ys = _swiglu_blocks(xs, bf(w_gate), bf(w_up), bf(w_down), blk_exp, n_used, tm, tf)
    return _combine_ln(x, ys, dest_flat, gates, ln_g, ln_b)


def kernel(x, l0_w_in, l0_conv_w, l0_w_br_conv, l0_w_br_attn, l0_pool_mix, l0_pool_scale, l0_w_br_pool, l0_w_o, l0_ln1_g, l0_ln1_b, l0_ffn_gate, l0_ffn_up, l0_ffn_down, l0_ln2_g, l0_ln2_b, l1_w_in, l1_conv_w, l1_w_br_conv, l1_w_br_attn, l1_pool_mix, l1_pool_scale, l1_w_br_pool, l1_w_o, l1_ln1_g, l1_ln1_b, l1_router, l1_exp_gate, l1_exp_up, l1_exp_down, l1_ln2_g, l1_ln2_b):
    batch, seq, d = x.shape
    assert seq == SEQ and d == D_MODEL
    rope_cos, rope_sin = _rope_tables(seq)
    x0 = x.reshape(batch * seq, d)
    x1, x1_bf = _mixer(x0, x0.astype(jnp.bfloat16), batch, l0_w_in, l0_conv_w, l0_w_br_conv, l0_w_br_attn,
                       l0_pool_mix, l0_pool_scale, l0_w_br_pool, l0_w_o, l0_ln1_g, l0_ln1_b, rope_cos, rope_sin)
    x2, x2_bf = _dense_ffn(x1, x1_bf, l0_ffn_gate, l0_ffn_up, l0_ffn_down, l0_ln2_g, l0_ln2_b)
    x3, _ = _mixer(x2, x2_bf, batch, l1_w_in, l1_conv_w, l1_w_br_conv, l1_w_br_attn,
                   l1_pool_mix, l1_pool_scale, l1_w_br_pool, l1_w_o, l1_ln1_g, l1_ln1_b, rope_cos, rope_sin)
    x4 = _moe_ffn(x3, l1_router, l1_exp_gate, l1_exp_up, l1_exp_down, l1_ln2_g, l1_ln2_b)
    return x4.reshape(batch, seq, d)
```

```python
import functools

import jax
import jax.numpy as jnp
from jax import lax
from jax.experimental import pallas as pl
from jax.experimental.pallas import tpu as pltpu

D_MODEL = 2048
SEQ = 4096
CONV_WIDTH = D_MODEL // 2
CONV_KERNEL = 3
HEAD_DIM = 128
HEADS_PER_GROUP = 4
GROUP_WIDTH = HEADS_PER_GROUP * HEAD_DIM
ATTN_PATTERNS = ((128, 1), (512, 4), (2048, 16))
N_GROUPS = len(ATTN_PATTERNS)
ATTN_HEADS = N_GROUPS * HEADS_PER_GROUP
QKV_WIDTH = 3 * N_GROUPS * GROUP_WIDTH
ATTN_BLOCK = 128
MAX_DILATION = max(d for _, d in ATTN_PATTERNS)
ROPE_THETA = 500000.0
ROPE_DIM = HEAD_DIM // 4
POOL_WINDOWS = (2, 4, 8, 16)
POOL_WIDTH = D_MODEL // 2
POOL_GROUP_DIM = POOL_WIDTH // len(POOL_WINDOWS)
POOL_HALO = 16
CONV_HALO = 8
D_FF = 7 * D_MODEL // 2
N_EXPERTS = 8
TOP_K = 2
LN_EPS = 1e-5
DEPTH = 2
DEEPNORM_ALPHA = (2.0 * DEPTH) ** 0.25
NEG_INF = -1e30
LANES = 128

VMEM_LIMIT = 56 * 1024 * 1024


def _params(sem, vmem=VMEM_LIMIT):
    return pltpu.CompilerParams(dimension_semantics=sem, vmem_limit_bytes=vmem)


def _proj_kernel(x_ref, w_ref, o_ref, wbf_ref):
    @pl.when(pl.program_id(1) == 0)
    def _():
        wbf_ref[...] = w_ref[...].astype(jnp.bfloat16)

    o_ref[...] = jnp.dot(x_ref[...], wbf_ref[...],
                         preferred_element_type=jnp.float32).astype(o_ref.dtype)


def _project(x_bf, w, col0, width, tn, out_dtype, tm=1024):
    m, k = x_bf.shape
    assert col0 % tn == 0 and width % tn == 0 and m % tm == 0
    j0 = col0 // tn
    return pl.pallas_call(
        _proj_kernel,
        grid=(width // tn, m // tm),
        in_specs=[pl.BlockSpec((tm, k), lambda j, i: (i, 0)),
                  pl.BlockSpec((k, tn), lambda j, i: (0, j0 + j))],
        out_specs=pl.BlockSpec((tm, tn), lambda j, i: (i, j)),
        out_shape=jax.ShapeDtypeStruct((m, width), out_dtype),
        scratch_shapes=[pltpu.VMEM((k, tn), jnp.bfloat16)],
        compiler_params=_params(("arbitrary", "arbitrary")),
        name="in_proj",
    )(x_bf, w)


def _rope(t, cos, sin):
    half = ROPE_DIM // 2
    lane = lax.broadcasted_iota(jnp.int32, t.shape, 1)
    partner = jnp.where(lane < half, pltpu.roll(t, HEAD_DIM - half, 1), pltpu.roll(t, half, 1))
    return t * cos + partner * sin


def _strided(start, size, stride):
    return pl.ds(start, size) if stride == 1 else pl.ds(start, size, stride=stride)


def _attn_kernel(q0_ref, k0_ref, v0_ref, q1_ref, k1_ref, v1_ref, q2_ref, k2_ref, v2_ref,
                 cos_ref, sin_ref, o_ref, q_s, k_s, v_s, out_s, lse_s, *, chunk, chains):
    groups = ((q0_ref, k0_ref, v0_ref), (q1_ref, k1_ref, v1_ref), (q2_ref, k2_ref, v2_ref))
    n_keys = 2 * ATTN_BLOCK
    a = lax.broadcasted_iota(jnp.int32, (ATTN_BLOCK, n_keys), 0)
    kk = lax.broadcasted_iota(jnp.int32, (ATTN_BLOCK, n_keys), 1)
    band = (kk >= a) & (kk <= a + ATTN_BLOCK)
    scale = HEAD_DIM ** -0.5

    for g, (q_ref, k_ref, v_ref) in enumerate(groups):
        d = ATTN_PATTERNS[g][1]
        pad = ATTN_BLOCK * d
        k_s[0:pad, :] = jnp.zeros((pad, HEAD_DIM), jnp.float32)
        v_s[0:pad, :] = jnp.zeros((pad, HEAD_DIM), jnp.float32)

        def stage(ci, carry, q_ref=q_ref, k_ref=k_ref, v_ref=v_ref, pad=pad):
            r0 = pl.multiple_of(ci * chunk, chunk)
            rows = pl.ds(r0, chunk)
            cos, sin = cos_ref[rows, :], sin_ref[rows, :]
            q_s[rows, :] = _rope(q_ref[0, rows, :].astype(jnp.float32), cos, sin)
            k_s[pl.ds(pad + r0, chunk), :] = _rope(k_ref[0, rows, :].astype(jnp.float32), cos, sin)
            v_s[pl.ds(pad + r0, chunk), :] = v_ref[0, rows, :].astype(jnp.float32)
            return carry

        lax.fori_loop(0, SEQ // chunk, stage, 0)

        blocks_per_residue = SEQ // d // ATTN_BLOCK
        unroll = min(chains, blocks_per_residue)
        residues_per_body = chains // unroll
        for c0 in range(0, d, residues_per_body):
            def block(blk, carry, c0=c0, d=d, g=g, residues_per_body=residues_per_body):
                base = pl.multiple_of(blk * (ATTN_BLOCK * d), ATTN_BLOCK)
                first_key = jnp.where(blk == 0, ATTN_BLOCK, 0)
                valid = band & (kk >= first_key)
                for c in range(c0, c0 + residues_per_body):
                    qb = q_s[_strided(base + c, ATTN_BLOCK, d), :].astype(jnp.bfloat16)
                    kb = k_s[_strided(base + c, n_keys, d), :].astype(jnp.bfloat16)
                    vb = v_s[_strided(base + c, n_keys, d), :].astype(jnp.bfloat16)
                    s = lax.dot_general(qb, kb, (((1,), (1,)), ((), ())),
                                        preferred_element_type=jnp.float32) * scale
                    s = jnp.where(valid, s, NEG_INF)
                    m = jnp.max(s, axis=-1, keepdims=True)
                    p = jnp.exp(s - m)
                    l = jnp.sum(p, axis=-1, keepdims=True)
                    o = jnp.dot(p.astype(jnp.bfloat16), vb, preferred_element_type=jnp.float32)
                    rows = _strided(base + c, ATTN_BLOCK, d)
                    out_s[g, rows, :] = o / l
                    lse_s[g, rows, :] = jnp.broadcast_to(m + jnp.log(l), (ATTN_BLOCK, HEAD_DIM))
                return carry

            lax.fori_loop(0, blocks_per_residue, block, 0, unroll=unroll)

    def combine(ci, carry):
        rows = pl.ds(pl.multiple_of(ci * chunk, chunk), chunk)
        l0, l1, l2 = lse_s[0, rows, :], lse_s[1, rows, :], lse_s[2, rows, :]
        mx = jnp.maximum(jnp.maximum(l0, l1), l2)
        e0, e1, e2 = jnp.exp(l0 - mx), jnp.exp(l1 - mx), jnp.exp(l2 - mx)
        num = e0 * out_s[0, rows, :] + e1 * out_s[1, rows, :] + e2 * out_s[2, rows, :]
        o_ref[0, rows, :] = (num / (e0 + e1 + e2)).astype(o_ref.dtype)
        return carry

    lax.fori_loop(0, SEQ // chunk, combine, 0)


def _attention(qkv, rope_cos, rope_sin, chunk=256, chains=4):
    b, s, _ = qkv.shape
    assert s == SEQ

    def head_block(part, group):
        col0 = part * ATTN_HEADS + group * HEADS_PER_GROUP
        return pl.BlockSpec((1, s, HEAD_DIM), lambda bi, h: (bi, 0, col0 + h))

    tab = pl.BlockSpec((s, LANES), lambda bi, h: (0, 0), pipeline_mode=pl.Buffered(1))
    in_specs = [head_block(part, group) for group in range(N_GROUPS) for part in range(3)]
    return pl.pallas_call(
        functools.partial(_attn_kernel, chunk=chunk, chains=chains),
        grid=(b, HEADS_PER_GROUP),
        in_specs=in_specs + [tab, tab],
        out_specs=pl.BlockSpec((1, s, HEAD_DIM), lambda bi, h: (bi, 0, h)),
        out_shape=jax.ShapeDtypeStruct((b, s, GROUP_WIDTH), jnp.bfloat16),
        scratch_shapes=[pltpu.VMEM((s, HEAD_DIM), jnp.float32),
                        pltpu.VMEM((s + ATTN_BLOCK * MAX_DILATION, HEAD_DIM), jnp.float32),
                        pltpu.VMEM((s + ATTN_BLOCK * MAX_DILATION, HEAD_DIM), jnp.float32),
                        pltpu.VMEM((N_GROUPS, s, HEAD_DIM), jnp.float32),
                        pltpu.VMEM((N_GROUPS, s, HEAD_DIM), jnp.float32)],
        compiler_params=_params(("arbitrary", "arbitrary")),
        name="dilated_attn",
    )(*([qkv] * 9), rope_cos, rope_sin)


def _branch_kernel(cb_ref, cc_ref, ch_ref, cch_ref, chh_ref, pu_ref, puh_ref, attn_ref,
                   g0_ref, g1_ref, g2_ref, convw_ref, pscale_ref,
                   wconv_ref, wattn_ref, wpool_ref, pmix_ref, o_ref, *, tm, tiles_per_seq):
    seq_tile = pl.program_id(0) % tiles_per_seq
    not_first = (seq_tile != 0).astype(jnp.float32)

    u = cc_ref[...] * ch_ref[...]
    uh = cch_ref[...] * chh_ref[...] * not_first
    ext = jnp.concatenate([uh, u], axis=0)
    u1 = pltpu.roll(ext, 1, 0)[CONV_HALO:]
    u2 = pltpu.roll(ext, 2, 0)[CONV_HALO:]
    cw = convw_ref[...]
    conv = u2 * cw[0:1] + u1 * cw[1:2]
    conv = conv + u * cw[2:3]
    ya_in = (cb_ref[...] * conv).astype(jnp.bfloat16)
    y_a = jnp.dot(ya_in, wconv_ref[...], preferred_element_type=jnp.float32)
    merged = jax.nn.sigmoid(g0_ref[...]) * y_a

    y_b = jnp.dot(attn_ref[...], wattn_ref[...], preferred_element_type=jnp.float32)
    merged = merged + jax.nn.sigmoid(g1_ref[...]) * y_b

    pu = pu_ref[...]
    pext = jnp.concatenate([puh_ref[...] * not_first, pu], axis=0)
    pos = seq_tile * tm + lax.broadcasted_iota(jnp.int32, (tm, 1), 0) + 1
    pscale = pscale_ref[...]
    y_c = None
    for gi, w in enumerate(POOL_WINDOWS):
        cols = slice(gi * POOL_GROUP_DIM, (gi + 1) * POOL_GROUP_DIM)
        run = pext[:, cols]
        step = 1
        while step < w:
            run = run + pltpu.roll(run, step, 0)
            step *= 2
        inv_cnt = 1.0 / jnp.minimum(pos, w).astype(jnp.float32)
        pooled = run[POOL_HALO:] * inv_cnt - pu[:, cols]
        mixed = jnp.dot(pooled.astype(jnp.bfloat16), pmix_ref[gi], preferred_element_type=jnp.float32)
        mixed = (mixed * pscale[:, cols]).astype(jnp.bfloat16)
        part = jnp.dot(mixed, wpool_ref[cols, :], preferred_element_type=jnp.float32)
        y_c = part if y_c is None else y_c + part
    merged = merged + jax.nn.sigmoid(g2_ref[...]) * y_c
    o_ref[...] = merged.astype(o_ref.dtype)


def _branches(conv_p, pool_p, attn, gates_p, conv_w, pool_scale,
              w_conv_bf, w_attn_bf, w_pool_bf, pool_mix_bf, tm=256):
    m = conv_p.shape[0]
    tiles_per_seq = SEQ // tm
    row = lambda width, col: pl.BlockSpec((tm, width), lambda i: (i, col))

    def halo(rows, col):
        per_tile = tm // rows
        return pl.BlockSpec((rows, CONV_WIDTH), lambda i: (jnp.maximum(i * per_tile - 1, 0), col))

    full = lambda shape: pl.BlockSpec(shape, lambda i: (0,) * len(shape), pipeline_mode=pl.Buffered(1))
    return pl.pallas_call(
        functools.partial(_branch_kernel, tm=tm, tiles_per_seq=tiles_per_seq),
        grid=(m // tm,),
        in_specs=[row(CONV_WIDTH, 0), row(CONV_WIDTH, 1), row(CONV_WIDTH, 2),
                  halo(CONV_HALO, 1), halo(CONV_HALO, 2),
                  row(POOL_WIDTH, 0), halo(POOL_HALO, 0),
                  row(GROUP_WIDTH, 0),
                  row(D_MODEL, 0), row(D_MODEL, 1), row(D_MODEL, 2),
                  full((CONV_KERNEL, CONV_WIDTH)), full((1, POOL_WIDTH)),
                  full((CONV_WIDTH, D_MODEL)), full((GROUP_WIDTH, D_MODEL)), full((POOL_WIDTH, D_MODEL)),
                  full((len(POOL_WINDOWS), POOL_GROUP_DIM, POOL_GROUP_DIM))],
        out_specs=pl.BlockSpec((tm, D_MODEL), lambda i: (i, 0)),
        out_shape=jax.ShapeDtypeStruct((m, D_MODEL), jnp.bfloat16),
        compiler_params=_params(("arbitrary",)),
        name="branches",
    )(conv_p, conv_p, conv_p, conv_p, conv_p, pool_p, pool_p, attn,
      gates_p, gates_p, gates_p, conv_w, pool_scale.reshape(1, POOL_WIDTH),
      w_conv_bf, w_attn_bf, w_pool_bf, pool_mix_bf)


def _resid_layer_norm(x, f, g, b):
    z = DEEPNORM_ALPHA * x + f
    mu = jnp.mean(z, axis=-1, keepdims=True)
    zc = z - mu
    var = jnp.mean(zc * zc, axis=-1, keepdims=True)
    return zc * lax.rsqrt(var + LN_EPS) * g + b


def _out_proj_ln_kernel(m_ref, w_ref, x_ref, g_ref, b_ref, o_ref, obf_ref):
    f = jnp.dot(m_ref[...], w_ref[...], preferred_element_type=jnp.float32)
    y = _resid_layer_norm(x_ref[...], f, g_ref[...], b_ref[...])
    o_ref[...] = y
    obf_ref[...] = y.astype(jnp.bfloat16)


def _out_proj_ln(merged_bf, w_o_bf, x, g, b, tm=512):
    m = x.shape[0]
    row = pl.BlockSpec((tm, D_MODEL), lambda i: (i, 0))
    vec = pl.BlockSpec((1, D_MODEL), lambda i: (0, 0))
    return pl.pallas_call(
        _out_proj_ln_kernel,
        grid=(m // tm,),
        in_specs=[row, pl.BlockSpec((D_MODEL, D_MODEL), lambda i: (0, 0)), row, vec, vec],
        out_specs=[row, row],
        out_shape=[jax.ShapeDtypeStruct((m, D_MODEL), jnp.float32),
                   jax.ShapeDtypeStruct((m, D_MODEL), jnp.bfloat16)],
        compiler_params=_params(("arbitrary",)),
        name="out_proj_ln",
    )(merged_bf, w_o_bf, x, g.reshape(1, D_MODEL), b.reshape(1, D_MODEL))


def _ln_kernel(x_ref, f_ref, g_ref, b_ref, o_ref, obf_ref):
    y = _resid_layer_norm(x_ref[...], f_ref[...], g_ref[...], b_ref[...])
    o_ref[...] = y
    obf_ref[...] = y.astype(jnp.bfloat16)


def _resid_ln(x, f, g, b, tm=512):
    m = x.shape[0]
    row = pl.BlockSpec((tm, D_MODEL), lambda i: (i, 0))
    vec = pl.BlockSpec((1, D_MODEL), lambda i: (0, 0))
    return pl.pallas_call(
        _ln_kernel,
        grid=(m // tm,),
        in_specs=[row, row, vec, vec],
        out_specs=[row, row],
        out_shape=[jax.ShapeDtypeStruct((m, D_MODEL), jnp.float32),
                   jax.ShapeDtypeStruct((m, D_MODEL), jnp.bfloat16)],
        compiler_params=_params(("arbitrary",)),
        name="resid_ln",
    )(x, f, g.reshape(1, D_MODEL), b.reshape(1, D_MODEL))


def _swiglu_kernel(blk_exp_ref, blk_rows_ref, n_used_ref, x_ref, wg_ref, wu_ref, wd_ref, o_ref,
                   xbf_ref, *, tm, sub):
    del blk_exp_ref, n_used_ref
    i, j = pl.program_id(0), pl.program_id(1)
    n_sub = (blk_rows_ref[i] + sub - 1) // sub

    @pl.when(j == 0)
    def _():
        xbf_ref[...] = x_ref[...].astype(jnp.bfloat16)
        o_ref[...] = jnp.zeros_like(o_ref)

    for v in range(1, tm // sub + 1):
        rows = v * sub

        @pl.when(n_sub == v)
        def _(rows=rows):
            x = xbf_ref[0:rows, :]
            gate = jnp.dot(x, wg_ref[0].astype(jnp.bfloat16), preferred_element_type=jnp.float32)
            up = jnp.dot(x, wu_ref[0].astype(jnp.bfloat16), preferred_element_type=jnp.float32)
            h = (gate * jax.nn.sigmoid(gate) * up).astype(jnp.bfloat16)
            o_ref[0:rows, :] += jnp.dot(h, wd_ref[0].astype(jnp.bfloat16),
                                        preferred_element_type=jnp.float32)


def _swiglu_blocks(x, w_gate, w_up, w_down, blk_exp, blk_rows, n_used, tm, tf, sub=256):
    r = x.shape[0]
    nf = D_FF // tf
    last = nf - 1

    def live(i, nu):
        return jnp.minimum(i, nu[0] - 1)

    def jj(i, j, nu):
        return jnp.where(i < nu[0], j, last)

    grid_spec = pltpu.PrefetchScalarGridSpec(
        num_scalar_prefetch=3,
        grid=(r // tm, nf),
        in_specs=[pl.BlockSpec((tm, D_MODEL), lambda i, j, be, br, nu: (live(i, nu), 0),
                               pipeline_mode=pl.Buffered(1)),
                  pl.BlockSpec((1, D_MODEL, tf), lambda i, j, be, br, nu: (be[live(i, nu)], 0, jj(i, j, nu))),
                  pl.BlockSpec((1, D_MODEL, tf), lambda i, j, be, br, nu: (be[live(i, nu)], 0, jj(i, j, nu))),
                  pl.BlockSpec((1, tf, D_MODEL), lambda i, j, be, br, nu: (be[live(i, nu)], jj(i, j, nu), 0))],
        out_specs=pl.BlockSpec((tm, D_MODEL), lambda i, j, be, br, nu: (i, 0),
                               pipeline_mode=pl.Buffered(1)),
        scratch_shapes=[pltpu.VMEM((tm, D_MODEL), jnp.bfloat16)],
    )
    return pl.pallas_call(
        functools.partial(_swiglu_kernel, tm=tm, sub=sub),
        grid_spec=grid_spec,
        out_shape=jax.ShapeDtypeStruct((r, D_MODEL), jnp.float32),
        compiler_params=_params(("arbitrary", "arbitrary")),
        name="swiglu_blocks",
    )(blk_exp, blk_rows, n_used, x, w_gate, w_up, w_down)


def _router_kernel(x_ref, w_ref, idx_ref, gate_ref, rank_ref, cnt_ref, carry_ref, *, tm):
    @pl.when(pl.program_id(0) == 0)
    def _():
        carry_ref[...] = jnp.zeros_like(carry_ref)

    logits = jnp.dot(x_ref[...], w_ref[...], preferred_element_type=jnp.float32,
                     precision=lax.Precision.HIGHEST)
    lane = lax.broadcasted_iota(jnp.int32, logits.shape, 1)
    logits = jnp.where(lane < N_EXPERTS, logits, -jnp.inf)
    m1 = jnp.max(logits, axis=-1, keepdims=True)
    i1 = jnp.min(jnp.where(logits == m1, lane, LANES), axis=-1, keepdims=True)
    rest = jnp.where(lane == i1, -jnp.inf, logits)
    m2 = jnp.max(rest, axis=-1, keepdims=True)
    i2 = jnp.min(jnp.where(rest == m2, lane, LANES), axis=-1, keepdims=True)
    e = jnp.exp(m2 - m1)
    den = 1.0 + e
    idx_ref[:, 0:1] = i1
    idx_ref[:, 1:2] = i2
    gate_ref[:, 0:1] = 1.0 / den
    gate_ref[:, 1:2] = e / den

    oh1 = (lane == i1).astype(jnp.float32)
    oh2 = (lane == i2).astype(jnp.float32)
    both = oh1 + oh2
    r = lax.broadcasted_iota(jnp.int32, (tm, tm), 0)
    c = lax.broadcasted_iota(jnp.int32, (tm, tm), 1)
    strict_lower = (r > c).astype(jnp.bfloat16)
    before = jnp.dot(strict_lower, both.astype(jnp.bfloat16), preferred_element_type=jnp.float32)
    before = before + carry_ref[...]
    rank_ref[:, 0:1] = jnp.sum(oh1 * before, axis=-1, keepdims=True).astype(jnp.int32)
    rank_ref[:, 1:2] = jnp.sum(oh2 * before, axis=-1, keepdims=True).astype(jnp.int32)
    carry_ref[...] += jnp.sum(both, axis=0, keepdims=True)
    cnt_ref[...] = carry_ref[...]


def _route(x, w_router, tm=512):
    m = x.shape[0]
    w_pad = jnp.pad(w_router, ((0, 0), (0, LANES - N_EXPERTS)))
    pair = pl.BlockSpec((tm, TOP_K), lambda i: (i, 0))
    return pl.pallas_call(
        functools.partial(_router_kernel, tm=tm),
        grid=(m // tm,),
        in_specs=[pl.BlockSpec((tm, D_MODEL), lambda i: (i, 0)),
                  pl.BlockSpec((D_MODEL, LANES), lambda i: (0, 0))],
        out_specs=[pair, pair, pair, pl.BlockSpec((1, LANES), lambda i: (0, 0))],
        out_shape=[jax.ShapeDtypeStruct((m, TOP_K), jnp.int32),
                   jax.ShapeDtypeStruct((m, TOP_K), jnp.float32),
                   jax.ShapeDtypeStruct((m, TOP_K), jnp.int32),
                   jax.ShapeDtypeStruct((1, LANES), jnp.float32)],
        scratch_shapes=[pltpu.VMEM((1, LANES), jnp.float32)],
        compiler_params=_params(("arbitrary",)),
        name="router",
    )(x, w_pad)


def _dispatch_kernel(dest_ref, x_ref, xs_in_ref, xs_ref, sem, *, tm):
    del xs_in_ref

    def issue(t, carry):
        for k in range(TOP_K):
            pltpu.make_async_copy(x_ref.at[pl.ds(t, 1)],
                                  xs_ref.at[pl.ds(dest_ref[TOP_K * t + k], 1)], sem).start()
        return carry

    lax.fori_loop(0, tm, issue, 0)
    for k in range(TOP_K):
        pltpu.make_async_copy(x_ref, xs_ref.at[pl.ds(0, tm)], sem).wait()


def _dispatch(x, dest_flat, rows, tm=256):
    m = x.shape[0]
    zeros = jnp.zeros((rows, D_MODEL), x.dtype)
    return pl.pallas_call(
        functools.partial(_dispatch_kernel, tm=tm),
        grid=(m // tm,),
        in_specs=[pl.BlockSpec((TOP_K * tm,), lambda i: (i,), memory_space=pltpu.SMEM),
                  pl.BlockSpec((tm, D_MODEL), lambda i: (i, 0)),
                  pl.BlockSpec(memory_space=pl.ANY)],
        out_specs=pl.BlockSpec(memory_space=pl.ANY),
        out_shape=jax.ShapeDtypeStruct((rows, D_MODEL), x.dtype),
        scratch_shapes=[pltpu.SemaphoreType.DMA(())],
        input_output_aliases={2: 0},
        compiler_params=_params(("arbitrary",)),
        name="dispatch",
    )(dest_flat, x, zeros)


def _combine_ln_kernel(dest_ref, x_ref, gate_ref, g_ref, b_ref, y_ref, o_ref, ybuf_ref, sem, *, tm):
    def issue(t, carry):
        for k in range(TOP_K):
            pltpu.make_async_copy(y_ref.at[pl.ds(dest_ref[TOP_K * t + k], 1)],
                                  ybuf_ref.at[k, pl.ds(t, 1)], sem).start()
        return carry

    lax.fori_loop(0, tm, issue, 0)
    for k in range(TOP_K):
        pltpu.make_async_copy(y_ref.at[pl.ds(0, tm)], ybuf_ref.at[k], sem).wait()
    gate = gate_ref[...]
    f = ybuf_ref[0] * gate[:, 0:1] + ybuf_ref[1] * gate[:, 1:2]
    o_ref[...] = _resid_layer_norm(x_ref[...], f, g_ref[...], b_ref[...])


def _combine_ln(x, y, dest_flat, gates, g, b, tm=256):
    m = x.shape[0]
    row = pl.BlockSpec((tm, D_MODEL), lambda i: (i, 0))
    vec = pl.BlockSpec((1, D_MODEL), lambda i: (0, 0))
    return pl.pallas_call(
        functools.partial(_combine_ln_kernel, tm=tm),
        grid=(m // tm,),
        in_specs=[pl.BlockSpec((TOP_K * tm,), lambda i: (i,), memory_space=pltpu.SMEM),
                  row, pl.BlockSpec((tm, TOP_K), lambda i: (i, 0)), vec, vec,
                  pl.BlockSpec(memory_space=pl.ANY)],
        out_specs=row,
        out_shape=jax.ShapeDtypeStruct((m, D_MODEL), jnp.float32),
        scratch_shapes=[pltpu.VMEM((TOP_K, tm, D_MODEL), jnp.float32),
                        pltpu.SemaphoreType.DMA(())],
        compiler_params=_params(("arbitrary",)),
        name="combine_ln",
    )(dest_flat, x, gates, g.reshape(1, D_MODEL), b.reshape(1, D_MODEL), y)


def _rope_tables(seq):
    inv_freq = ROPE_THETA ** (-jnp.arange(0, ROPE_DIM, 2, dtype=jnp.float32) / ROPE_DIM)
    ang = jnp.arange(seq, dtype=jnp.float32)[:, None] * inv_freq[None, :]
    cos, sin = jnp.cos(ang), jnp.sin(ang)
    pad = HEAD_DIM - ROPE_DIM
    cos_full = jnp.concatenate([cos, cos, jnp.ones((seq, pad), jnp.float32)], axis=1)
    sin_full = jnp.concatenate([-sin, sin, jnp.zeros((seq, pad), jnp.float32)], axis=1)
    return cos_full, sin_full


def _mixer(x, x_bf, batch, w_in, conv_w, w_br_conv, w_br_attn, pool_mix, pool_scale, w_br_pool, w_o,
           ln_g, ln_b, rope_cos, rope_sin):
    conv_cols = CONV_KERNEL * CONV_WIDTH
    conv_p = _project(x_bf, w_in, 0, conv_cols, 1024, jnp.float32)
    qkv = _project(x_bf, w_in, conv_cols, QKV_WIDTH, 1536, jnp.bfloat16)
    pool_p = _project(x_bf, w_in, conv_cols + QKV_WIDTH, POOL_WIDTH, 512, jnp.float32)
    gates_p = _project(x_bf, w_in, conv_cols + QKV_WIDTH + POOL_WIDTH, N_GROUPS * D_MODEL, 512, jnp.float32)
    attn = _attention(qkv.reshape(batch, SEQ, QKV_WIDTH), rope_cos, rope_sin)
    attn = attn.reshape(batch * SEQ, GROUP_WIDTH)
    bf = lambda w: w.astype(jnp.bfloat16)
    merged = _branches(conv_p, pool_p, attn, gates_p, conv_w, pool_scale,
                       bf(w_br_conv), bf(w_br_attn), bf(w_br_pool), bf(pool_mix))
    return _out_proj_ln(merged, bf(w_o), x, ln_g, ln_b)


def _dense_ffn(x, x_bf, w_gate, w_up, w_down, ln_g, ln_b, tm=1024, tf=512):
    n_blk = x.shape[0] // tm
    blk_exp = jnp.zeros((n_blk,), jnp.int32)
    blk_rows = jnp.full((n_blk,), tm, jnp.int32)
    n_used = jnp.full((1,), n_blk, jnp.int32)
    f = _swiglu_blocks(x_bf, w_gate[None], w_up[None], w_down[None], blk_exp, blk_rows, n_used, tm, tf)
    return _resid_ln(x, f, ln_g, ln_b)


def _moe_ffn(x, w_router, w_gate, w_up, w_down, ln_g, ln_b, tm=1024, tf=512):
    m = x.shape[0]
    idx, gates, rank, cnt = _route(x, w_router)
    counts = cnt[0, :N_EXPERTS].astype(jnp.int32)
    padded = (counts + tm - 1) // tm * tm
    pend = jnp.cumsum(padded)
    pstart = pend - padded
    dest_flat = (pstart[idx] + rank).reshape(-1)
    n_blk = m * TOP_K // tm + N_EXPERTS
    blk_start = jnp.arange(n_blk, dtype=jnp.int32) * tm
    blk_exp = jnp.minimum(jnp.sum(blk_start[:, None] >= pend[None, :], axis=1), N_EXPERTS - 1).astype(jnp.int32)
    blk_rows = jnp.clip(pstart[blk_exp] + counts[blk_exp] - blk_start, 0, tm).astype(jnp.int32)
    n_used = (pend[-1:] // tm).astype(jnp.int32)
    blk_rows = jnp.where(jnp.arange(n_blk) < n_used[0], blk_rows, 0)
    xs = _dispatch(x, dest_flat, n_blk * tm)
    ys = _swiglu_blocks(xs, w_gate, w_up, w_down, blk_exp, blk_rows, n_used, tm, tf)
    return _combine_ln(x, ys, dest_flat, gates, ln_g, ln_b)


def kernel(x, l0_w_in, l0_conv_w, l0_w_br_conv, l0_w_br_attn, l0_pool_mix, l0_pool_scale, l0_w_br_pool, l0_w_o, l0_ln1_g, l0_ln1_b, l0_ffn_gate, l0_ffn_up, l0_ffn_down, l0_ln2_g, l0_ln2_b, l1_w_in, l1_conv_w, l1_w_br_conv, l1_w_br_attn, l1_pool_mix, l1_pool_scale, l1_w_br_pool, l1_w_o, l1_ln1_g, l1_ln1_b, l1_router, l1_exp_gate, l1_exp_up, l1_exp_down, l1_ln2_g, l1_ln2_b):
    batch, seq, d = x.shape
    assert seq == SEQ and d == D_MODEL
    rope_cos, rope_sin = _rope_tables(seq)
    x0 = x.reshape(batch * seq, d)
    x1, x1_bf = _mixer(x0, x0.astype(jnp.bfloat16), batch, l0_w_in, l0_conv_w, l0_w_br_conv, l0_w_br_attn,
                       l0_pool_mix, l0_pool_scale, l0_w_br_pool, l0_w_o, l0_ln1_g, l0_ln1_b, rope_cos, rope_sin)
    x2, x2_bf = _dense_ffn(x1, x1_bf, l0_ffn_gate, l0_ffn_up, l0_ffn_down, l0_ln2_g, l0_ln2_b)
    x3, _ = _mixer(x2, x2_bf, batch, l1_w_in, l1_conv_w, l1_w_br_conv, l1_w_br_attn,
                   l1_pool_mix, l1_pool_scale, l1_w_br_pool, l1_w_o, l1_ln1_g, l1_ln1_b, rope_cos, rope_sin)
    x4 = _moe_ffn(x3, l1_router, l1_exp_gate, l1_exp_up, l1_exp_down, l1_ln2_g, l1_ln2_b)
    return x4.reshape(batch, seq, d)
```

```python
import functools

import jax
import jax.numpy as jnp
from jax import lax
from jax.experimental import pallas as pl
from jax.experimental.pallas import tpu as pltpu

D_MODEL = 2048
SEQ = 4096
CONV_WIDTH = D_MODEL // 2
CONV_KERNEL = 3
HEAD_DIM = 128
HEADS_PER_GROUP = 4
GROUP_WIDTH = HEADS_PER_GROUP * HEAD_DIM
ATTN_PATTERNS = ((128, 1), (512, 4), (2048, 16))
N_GROUPS = len(ATTN_PATTERNS)
ATTN_HEADS = N_GROUPS * HEADS_PER_GROUP
QKV_WIDTH = 3 * N_GROUPS * GROUP_WIDTH
ATTN_BLOCK = 128
MAX_DILATION = max(d for _, d in ATTN_PATTERNS)
ROPE_THETA = 500000.0
ROPE_DIM = HEAD_DIM // 4
POOL_WINDOWS = (2, 4, 8, 16)
POOL_WIDTH = D_MODEL // 2
POOL_GROUP_DIM = POOL_WIDTH // len(POOL_WINDOWS)
POOL_HALO = 16
CONV_HALO = 16
D_FF = 7 * D_MODEL // 2
N_EXPERTS = 8
TOP_K = 2
LN_EPS = 1e-5
DEPTH = 2
DEEPNORM_ALPHA = (2.0 * DEPTH) ** 0.25
NEG_INF = -1e30
LANES = 128

VMEM_LIMIT = 56 * 1024 * 1024


def _params(sem, vmem=VMEM_LIMIT):
    return pltpu.CompilerParams(dimension_semantics=sem, vmem_limit_bytes=vmem)


def _proj_kernel(x_ref, w_ref, o_ref, wbf_ref):
    @pl.when(pl.program_id(1) == 0)
    def _():
        wbf_ref[...] = w_ref[...].astype(jnp.bfloat16)

    o_ref[...] = jnp.dot(x_ref[...], wbf_ref[...],
                         preferred_element_type=jnp.float32).astype(o_ref.dtype)


def _project(x_bf, w, col0, width, tn, out_dtype, tm=1024):
    m, k = x_bf.shape
    assert col0 % tn == 0 and width % tn == 0 and m % tm == 0
    j0 = col0 // tn
    return pl.pallas_call(
        _proj_kernel,
        grid=(width // tn, m // tm),
        in_specs=[pl.BlockSpec((tm, k), lambda j, i: (i, 0)),
                  pl.BlockSpec((k, tn), lambda j, i: (0, j0 + j))],
        out_specs=pl.BlockSpec((tm, tn), lambda j, i: (i, j)),
        out_shape=jax.ShapeDtypeStruct((m, width), out_dtype),
        scratch_shapes=[pltpu.VMEM((k, tn), jnp.bfloat16)],
        compiler_params=_params(("arbitrary", "arbitrary")),
        name="in_proj",
    )(x_bf, w)


def _rope(t, cos, sin):
    half = ROPE_DIM // 2
    lane = lax.broadcasted_iota(jnp.int32, t.shape, 1)
    partner = jnp.where(lane < half, pltpu.roll(t, HEAD_DIM - half, 1), pltpu.roll(t, half, 1))
    return t * cos + partner * sin


def _strided(start, size, stride):
    return pl.ds(start, size) if stride == 1 else pl.ds(start, size, stride=stride)


def _attn_kernel(q0_ref, k0_ref, v0_ref, q1_ref, k1_ref, v1_ref, q2_ref, k2_ref, v2_ref,
                 cos_ref, sin_ref, o_ref, q_s, k_s, v_s, out_s, lse_s, *, chunk, chains):
    groups = ((q0_ref, k0_ref, v0_ref), (q1_ref, k1_ref, v1_ref), (q2_ref, k2_ref, v2_ref))
    n_keys = 2 * ATTN_BLOCK
    a = lax.broadcasted_iota(jnp.int32, (ATTN_BLOCK, n_keys), 0)
    kk = lax.broadcasted_iota(jnp.int32, (ATTN_BLOCK, n_keys), 1)
    band = (kk >= a) & (kk <= a + ATTN_BLOCK)
    scale = HEAD_DIM ** -0.5

    for g, (q_ref, k_ref, v_ref) in enumerate(groups):
        d = ATTN_PATTERNS[g][1]
        pad = ATTN_BLOCK * d
        k_s[0:pad, :] = jnp.zeros((pad, HEAD_DIM), jnp.float32)
        v_s[0:pad, :] = jnp.zeros((pad, HEAD_DIM), jnp.float32)

        def stage(ci, carry, q_ref=q_ref, k_ref=k_ref, v_ref=v_ref, pad=pad):
            r0 = pl.multiple_of(ci * chunk, chunk)
            rows = pl.ds(r0, chunk)
            cos, sin = cos_ref[rows, :], sin_ref[rows, :]
            q_s[rows, :] = _rope(q_ref[0, rows, :].astype(jnp.float32), cos, sin)
            k_s[pl.ds(pad + r0, chunk), :] = _rope(k_ref[0, rows, :].astype(jnp.float32), cos, sin)
            v_s[pl.ds(pad + r0, chunk), :] = v_ref[0, rows, :].astype(jnp.float32)
            return carry

        lax.fori_loop(0, SEQ // chunk, stage, 0)

        blocks_per_residue = SEQ // d // ATTN_BLOCK
        unroll = min(chains, blocks_per_residue)
        residues_per_body = chains // unroll
        for c0 in range(0, d, residues_per_body):
            def block(blk, carry, c0=c0, d=d, g=g, residues_per_body=residues_per_body):
                base = pl.multiple_of(blk * (ATTN_BLOCK * d), ATTN_BLOCK)
                first_key = jnp.where(blk == 0, ATTN_BLOCK, 0)
                valid = band & (kk >= first_key)
                for c in range(c0, c0 + residues_per_body):
                    qb = q_s[_strided(base + c, ATTN_BLOCK, d), :].astype(jnp.bfloat16)
                    kb = k_s[_strided(base + c, n_keys, d), :].astype(jnp.bfloat16)
                    vb = v_s[_strided(base + c, n_keys, d), :].astype(jnp.bfloat16)
                    s = lax.dot_general(qb, kb, (((1,), (1,)), ((), ())),
                                        preferred_element_type=jnp.float32) * scale
                    s = jnp.where(valid, s, NEG_INF)
                    m = jnp.max(s, axis=-1, keepdims=True)
                    p = jnp.exp(s - m)
                    l = jnp.sum(p, axis=-1, keepdims=True)
                    o = jnp.dot(p.astype(jnp.bfloat16), vb, preferred_element_type=jnp.float32)
                    rows = _strided(base + c, ATTN_BLOCK, d)
                    out_s[g, rows, :] = o / l
                    lse_s[g, rows, :] = jnp.broadcast_to(m + jnp.log(l), (ATTN_BLOCK, HEAD_DIM))
                return carry

            lax.fori_loop(0, blocks_per_residue, block, 0, unroll=unroll)

    def combine(ci, carry):
        rows = pl.ds(pl.multiple_of(ci * chunk, chunk), chunk)
        l0, l1, l2 = lse_s[0, rows, :], lse_s[1, rows, :], lse_s[2, rows, :]
        mx = jnp.maximum(jnp.maximum(l0, l1), l2)
        e0, e1, e2 = jnp.exp(l0 - mx), jnp.exp(l1 - mx), jnp.exp(l2 - mx)
        num = e0 * out_s[0, rows, :] + e1 * out_s[1, rows, :] + e2 * out_s[2, rows, :]
        o_ref[0, rows, :] = (num / (e0 + e1 + e2)).astype(o_ref.dtype)
        return carry

    lax.fori_loop(0, SEQ // chunk, combine, 0)


def _attention(qkv, rope_cos, rope_sin, chunk=256, chains=4):
    b, s, _ = qkv.shape
    assert s == SEQ

    def head_block(part, group):
        col0 = part * ATTN_HEADS + group * HEADS_PER_GROUP
        return pl.BlockSpec((1, s, HEAD_DIM), lambda bi, h: (bi, 0, col0 + h))

    tab = pl.BlockSpec((s, LANES), lambda bi, h: (0, 0), pipeline_mode=pl.Buffered(1))
    in_specs = [head_block(part, group) for group in range(N_GROUPS) for part in range(3)]
    return pl.pallas_call(
        functools.partial(_attn_kernel, chunk=chunk, chains=chains),
        grid=(b, HEADS_PER_GROUP),
        in_specs=in_specs + [tab, tab],
        out_specs=pl.BlockSpec((1, s, HEAD_DIM), lambda bi, h: (bi, 0, h)),
        out_shape=jax.ShapeDtypeStruct((b, s, GROUP_WIDTH), jnp.bfloat16),
        scratch_shapes=[pltpu.VMEM((s, HEAD_DIM), jnp.float32),
                        pltpu.VMEM((s + ATTN_BLOCK * MAX_DILATION, HEAD_DIM), jnp.float32),
                        pltpu.VMEM((s + ATTN_BLOCK * MAX_DILATION, HEAD_DIM), jnp.float32),
                        pltpu.VMEM((N_GROUPS, s, HEAD_DIM), jnp.float32),
                        pltpu.VMEM((N_GROUPS, s, HEAD_DIM), jnp.float32)],
        compiler_params=_params(("arbitrary", "arbitrary")),
        name="dilated_attn",
    )(*([qkv] * 9), rope_cos, rope_sin)


def _branch_kernel(cb_ref, cc_ref, ch_ref, cch_ref, chh_ref, pu_ref, puh_ref, attn_ref,
                   g0_ref, g1_ref, g2_ref, convw_ref, pscale_ref,
                   wconv_ref, wattn_ref, wpool_ref, pmix_ref, o_ref, *, tm, tiles_per_seq):
    seq_tile = pl.program_id(0) % tiles_per_seq
    not_first = (seq_tile != 0).astype(jnp.float32)

    f32 = lambda ref: ref[...].astype(jnp.float32)
    u = f32(cc_ref) * f32(ch_ref)
    uh = f32(cch_ref) * f32(chh_ref) * not_first
    ext = jnp.concatenate([uh, u], axis=0)
    u1 = pltpu.roll(ext, 1, 0)[CONV_HALO:]
    u2 = pltpu.roll(ext, 2, 0)[CONV_HALO:]
    cw = convw_ref[...]
    conv = u2 * cw[0:1] + u1 * cw[1:2]
    conv = conv + u * cw[2:3]
    ya_in = (f32(cb_ref) * conv).astype(jnp.bfloat16)
    y_a = jnp.dot(ya_in, wconv_ref[...], preferred_element_type=jnp.float32)
    merged = jax.nn.sigmoid(f32(g0_ref)) * y_a

    y_b = jnp.dot(attn_ref[...], wattn_ref[...], preferred_element_type=jnp.float32)
    merged = merged + jax.nn.sigmoid(f32(g1_ref)) * y_b

    pu = f32(pu_ref)
    pext = jnp.concatenate([f32(puh_ref) * not_first, pu], axis=0)
    pos = seq_tile * tm + lax.broadcasted_iota(jnp.int32, (tm, 1), 0) + 1
    pscale = pscale_ref[...]
    y_c = None
    for gi, w in enumerate(POOL_WINDOWS):
        cols = slice(gi * POOL_GROUP_DIM, (gi + 1) * POOL_GROUP_DIM)
        run = pext[:, cols]
        step = 1
        while step < w:
            run = run + pltpu.roll(run, step, 0)
            step *= 2
        inv_cnt = 1.0 / jnp.minimum(pos, w).astype(jnp.float32)
        pooled = run[POOL_HALO:] * inv_cnt - pu[:, cols]
        mixed = jnp.dot(pooled.astype(jnp.bfloat16), pmix_ref[gi], preferred_element_type=jnp.float32)
        mixed = (mixed * pscale[:, cols]).astype(jnp.bfloat16)
        part = jnp.dot(mixed, wpool_ref[cols, :], preferred_element_type=jnp.float32)
        y_c = part if y_c is None else y_c + part
    merged = merged + jax.nn.sigmoid(f32(g2_ref)) * y_c
    o_ref[...] = merged.astype(o_ref.dtype)


def _branches(conv_p, gates_pool_p, attn, conv_w, pool_scale,
              w_conv_bf, w_attn_bf, w_pool_bf, pool_mix_bf, tm=256):
    m = conv_p.shape[0]
    pool_col = N_GROUPS * D_MODEL // POOL_WIDTH
    tiles_per_seq = SEQ // tm
    row = lambda width, col: pl.BlockSpec((tm, width), lambda i: (i, col))

    def halo(rows, col):
        per_tile = tm // rows
        return pl.BlockSpec((rows, CONV_WIDTH), lambda i: (jnp.maximum(i * per_tile - 1, 0), col))

    full = lambda shape: pl.BlockSpec(shape, lambda i: (0,) * len(shape), pipeline_mode=pl.Buffered(1))
    return pl.pallas_call(
        functools.partial(_branch_kernel, tm=tm, tiles_per_seq=tiles_per_seq),
        grid=(m // tm,),
        in_specs=[row(CONV_WIDTH, 0), row(CONV_WIDTH, 1), row(CONV_WIDTH, 2),
                  halo(CONV_HALO, 1), halo(CONV_HALO, 2),
                  row(POOL_WIDTH, pool_col), halo(POOL_HALO, pool_col),
                  row(GROUP_WIDTH, 0),
                  row(D_MODEL, 0), row(D_MODEL, 1), row(D_MODEL, 2),
                  full((CONV_KERNEL, CONV_WIDTH)), full((1, POOL_WIDTH)),
                  full((CONV_WIDTH, D_MODEL)), full((GROUP_WIDTH, D_MODEL)), full((POOL_WIDTH, D_MODEL)),
                  full((len(POOL_WINDOWS), POOL_GROUP_DIM, POOL_GROUP_DIM))],
        out_specs=pl.BlockSpec((tm, D_MODEL), lambda i: (i, 0)),
        out_shape=jax.ShapeDtypeStruct((m, D_MODEL), jnp.bfloat16),
        compiler_params=_params(("arbitrary",)),
        name="branches",
    )(conv_p, conv_p, conv_p, conv_p, conv_p, gates_pool_p, gates_pool_p, attn,
      gates_pool_p, gates_pool_p, gates_pool_p, conv_w, pool_scale.reshape(1, POOL_WIDTH),
      w_conv_bf, w_attn_bf, w_pool_bf, pool_mix_bf)


def _resid_layer_norm(x, f, g, b):
    z = DEEPNORM_ALPHA * x + f
    mu = jnp.mean(z, axis=-1, keepdims=True)
    zc = z - mu
    var = jnp.mean(zc * zc, axis=-1, keepdims=True)
    return zc * lax.rsqrt(var + LN_EPS) * g + b


def _out_proj_ln_kernel(m_ref, w_ref, x_ref, g_ref, b_ref, o_ref, obf_ref):
    f = jnp.dot(m_ref[...], w_ref[...], preferred_element_type=jnp.float32)
    y = _resid_layer_norm(x_ref[...], f, g_ref[...], b_ref[...])
    o_ref[...] = y
    obf_ref[...] = y.astype(jnp.bfloat16)


def _out_proj_ln(merged_bf, w_o_bf, x, g, b, tm=512):
    m = x.shape[0]
    row = pl.BlockSpec((tm, D_MODEL), lambda i: (i, 0))
    vec = pl.BlockSpec((1, D_MODEL), lambda i: (0, 0))
    return pl.pallas_call(
        _out_proj_ln_kernel,
        grid=(m // tm,),
        in_specs=[row, pl.BlockSpec((D_MODEL, D_MODEL), lambda i: (0, 0)), row, vec, vec],
        out_specs=[row, row],
        out_shape=[jax.ShapeDtypeStruct((m, D_MODEL), jnp.float32),
                   jax.ShapeDtypeStruct((m, D_MODEL), jnp.bfloat16)],
        compiler_params=_params(("arbitrary",)),
        name="out_proj_ln",
    )(merged_bf, w_o_bf, x, g.reshape(1, D_MODEL), b.reshape(1, D_MODEL))


def _ln_kernel(x_ref, f_ref, g_ref, b_ref, o_ref, obf_ref):
    y = _resid_layer_norm(x_ref[...], f_ref[...], g_ref[...], b_ref[...])
    o_ref[...] = y
    obf_ref[...] = y.astype(jnp.bfloat16)


def _resid_ln(x, f, g, b, tm=512):
    m = x.shape[0]
    row = pl.BlockSpec((tm, D_MODEL), lambda i: (i, 0))
    vec = pl.BlockSpec((1, D_MODEL), lambda i: (0, 0))
    return pl.pallas_call(
        _ln_kernel,
        grid=(m // tm,),
        in_specs=[row, row, vec, vec],
        out_specs=[row, row],
        out_shape=[jax.ShapeDtypeStruct((m, D_MODEL), jnp.float32),
                   jax.ShapeDtypeStruct((m, D_MODEL), jnp.bfloat16)],
        compiler_params=_params(("arbitrary",)),
        name="resid_ln",
    )(x, f, g.reshape(1, D_MODEL), b.reshape(1, D_MODEL))


def _swiglu_kernel(blk_exp_ref, blk_rows_ref, n_used_ref, x_ref, wg_ref, wu_ref, wd_ref, o_ref,
                   xbf_ref, *, tm, sub):
    del blk_exp_ref, n_used_ref
    i, j = pl.program_id(0), pl.program_id(1)
    n_sub = (blk_rows_ref[i] + sub - 1) // sub

    @pl.when(j == 0)
    def _():
        xbf_ref[...] = x_ref[...].astype(jnp.bfloat16)
        o_ref[...] = jnp.zeros_like(o_ref)

    for v in range(1, tm // sub + 1):
        rows = v * sub

        @pl.when(n_sub == v)
        def _(rows=rows):
            x = xbf_ref[0:rows, :]
            gate = jnp.dot(x, wg_ref[0].astype(jnp.bfloat16), preferred_element_type=jnp.float32)
            up = jnp.dot(x, wu_ref[0].astype(jnp.bfloat16), preferred_element_type=jnp.float32)
            h = (gate * jax.nn.sigmoid(gate) * up).astype(jnp.bfloat16)
            o_ref[0:rows, :] += jnp.dot(h, wd_ref[0].astype(jnp.bfloat16),
                                        preferred_element_type=jnp.float32)


def _swiglu_blocks(x, w_gate, w_up, w_down, blk_exp, blk_rows, n_used, tm, tf, sub=256):
    r = x.shape[0]
    nf = D_FF // tf
    last = nf - 1

    def live(i, nu):
        return jnp.maximum(jnp.minimum(i, nu[0] - 1), 0)

    def jj(i, j, nu):
        return jnp.where(i < nu[0], j, last)

    grid_spec = pltpu.PrefetchScalarGridSpec(
        num_scalar_prefetch=3,
        grid=(r // tm, nf),
        in_specs=[pl.BlockSpec((tm, D_MODEL), lambda i, j, be, br, nu: (live(i, nu), 0),
                               pipeline_mode=pl.Buffered(1)),
                  pl.BlockSpec((1, D_MODEL, tf), lambda i, j, be, br, nu: (be[live(i, nu)], 0, jj(i, j, nu))),
                  pl.BlockSpec((1, D_MODEL, tf), lambda i, j, be, br, nu: (be[live(i, nu)], 0, jj(i, j, nu))),
                  pl.BlockSpec((1, tf, D_MODEL), lambda i, j, be, br, nu: (be[live(i, nu)], jj(i, j, nu), 0))],
        out_specs=pl.BlockSpec((tm, D_MODEL), lambda i, j, be, br, nu: (i, 0),
                               pipeline_mode=pl.Buffered(1)),
        scratch_shapes=[pltpu.VMEM((tm, D_MODEL), jnp.bfloat16)],
    )
    return pl.pallas_call(
        functools.partial(_swiglu_kernel, tm=tm, sub=sub),
        grid_spec=grid_spec,
        out_shape=jax.ShapeDtypeStruct((r, D_MODEL), jnp.float32),
        compiler_params=_params(("arbitrary", "arbitrary")),
        name="swiglu_blocks",
    )(blk_exp, blk_rows, n_used, x, w_gate, w_up, w_down)


def _router_kernel(x_ref, w_ref, idx_ref, gate_ref, rank_ref, cnt_ref, carry_ref, *, tm):
    @pl.when(pl.program_id(0) == 0)
    def _():
        carry_ref[...] = jnp.zeros_like(carry_ref)

    logits = jnp.dot(x_ref[...], w_ref[...], preferred_element_type=jnp.float32,
                     precision=lax.Precision.HIGHEST)
    lane = lax.broadcasted_iota(jnp.int32, logits.shape, 1)
    logits = jnp.where(lane < N_EXPERTS, logits, -jnp.inf)
    m1 = jnp.max(logits, axis=-1, keepdims=True)
    i1 = jnp.min(jnp.where(logits == m1, lane, LANES), axis=-1, keepdims=True)
    rest = jnp.where(lane == i1, -jnp.inf, logits)
    m2 = jnp.max(rest, axis=-1, keepdims=True)
    i2 = jnp.min(jnp.where(rest == m2, lane, LANES), axis=-1, keepdims=True)
    e = jnp.exp(m2 - m1)
    den = 1.0 + e
    idx_ref[:, 0:1] = i1
    idx_ref[:, 1:2] = i2
    gate_ref[:, 0:1] = 1.0 / den
    gate_ref[:, 1:2] = e / den

    oh1 = (lane == i1).astype(jnp.float32)
    oh2 = (lane == i2).astype(jnp.float32)
    both = oh1 + oh2
    r = lax.broadcasted_iota(jnp.int32, (tm, tm), 0)
    c = lax.broadcasted_iota(jnp.int32, (tm, tm), 1)
    strict_lower = (r > c).astype(jnp.bfloat16)
    before = jnp.dot(strict_lower, both.astype(jnp.bfloat16), preferred_element_type=jnp.float32)
    before = before + carry_ref[...]
    rank_ref[:, 0:1] = jnp.sum(oh1 * before, axis=-1, keepdims=True).astype(jnp.int32)
    rank_ref[:, 1:2] = jnp.sum(oh2 * before, axis=-1, keepdims=True).astype(jnp.int32)
    carry_ref[...] += jnp.sum(both, axis=0, keepdims=True)
    cnt_ref[...] = carry_ref[...]


def _route(x, w_router, tm=512):
    m = x.shape[0]
    w_pad = jnp.pad(w_router, ((0, 0), (0, LANES - N_EXPERTS)))
    pair = pl.BlockSpec((tm, TOP_K), lambda i: (i, 0))
    return pl.pallas_call(
        functools.partial(_router_kernel, tm=tm),
        grid=(m // tm,),
        in_specs=[pl.BlockSpec((tm, D_MODEL), lambda i: (i, 0)),
                  pl.BlockSpec((D_MODEL, LANES), lambda i: (0, 0))],
        out_specs=[pair, pair, pair, pl.BlockSpec((1, LANES), lambda i: (0, 0))],
        out_shape=[jax.ShapeDtypeStruct((m, TOP_K), jnp.int32),
                   jax.ShapeDtypeStruct((m, TOP_K), jnp.float32),
                   jax.ShapeDtypeStruct((m, TOP_K), jnp.int32),
                   jax.ShapeDtypeStruct((1, LANES), jnp.float32)],
        scratch_shapes=[pltpu.VMEM((1, LANES), jnp.float32)],
        compiler_params=_params(("arbitrary",)),
        name="router",
    )(x, w_pad)


def _dispatch_kernel(dest_ref, x_ref, xs_in_ref, xs_ref, sem, *, tm):
    del xs_in_ref

    def issue(t, carry):
        for k in range(TOP_K):
            pltpu.make_async_copy(x_ref.at[pl.ds(t, 1)],
                                  xs_ref.at[pl.ds(dest_ref[TOP_K * t + k], 1)], sem).start()
        return carry

    lax.fori_loop(0, tm, issue, 0, unroll=8)
    for k in range(TOP_K):
        pltpu.make_async_copy(x_ref, xs_ref.at[pl.ds(0, tm)], sem).wait()


def _dispatch(x, dest_flat, rows, tm=256):
    m = x.shape[0]
    zeros = jnp.zeros((rows, D_MODEL), x.dtype)
    return pl.pallas_call(
        functools.partial(_dispatch_kernel, tm=tm),
        grid=(m // tm,),
        in_specs=[pl.BlockSpec((TOP_K * tm,), lambda i: (i,), memory_space=pltpu.SMEM),
                  pl.BlockSpec((tm, D_MODEL), lambda i: (i, 0)),
                  pl.BlockSpec(memory_space=pl.ANY)],
        out_specs=pl.BlockSpec(memory_space=pl.ANY),
        out_shape=jax.ShapeDtypeStruct((rows, D_MODEL), x.dtype),
        scratch_shapes=[pltpu.SemaphoreType.DMA(())],
        input_output_aliases={2: 0},
        compiler_params=_params(("arbitrary",)),
        name="dispatch",
    )(dest_flat, x, zeros)


def _combine_ln_kernel(dest_ref, x_ref, gate_ref, g_ref, b_ref, y_ref, o_ref, ybuf_ref, sem, *, tm):
    def issue(t, carry):
        for k in range(TOP_K):
            pltpu.make_async_copy(y_ref.at[pl.ds(dest_ref[TOP_K * t + k], 1)],
                                  ybuf_ref.at[k, pl.ds(t, 1)], sem).start()
        return carry

    lax.fori_loop(0, tm, issue, 0, unroll=8)
    for k in range(TOP_K):
        pltpu.make_async_copy(y_ref.at[pl.ds(0, tm)], ybuf_ref.at[k], sem).wait()
    gate = gate_ref[...]
    f = ybuf_ref[0] * gate[:, 0:1] + ybuf_ref[1] * gate[:, 1:2]
    o_ref[...] = _resid_layer_norm(x_ref[...], f, g_ref[...], b_ref[...])


def _combine_ln(x, y, dest_flat, gates, g, b, tm=256):
    m = x.shape[0]
    row = pl.BlockSpec((tm, D_MODEL), lambda i: (i, 0))
    vec = pl.BlockSpec((1, D_MODEL), lambda i: (0, 0))
    return pl.pallas_call(
        functools.partial(_combine_ln_kernel, tm=tm),
        grid=(m // tm,),
        in_specs=[pl.BlockSpec((TOP_K * tm,), lambda i: (i,), memory_space=pltpu.SMEM),
                  row, pl.BlockSpec((tm, TOP_K), lambda i: (i, 0)), vec, vec,
                  pl.BlockSpec(memory_space=pl.ANY)],
        out_specs=row,
        out_shape=jax.ShapeDtypeStruct((m, D_MODEL), jnp.float32),
        scratch_shapes=[pltpu.VMEM((TOP_K, tm, D_MODEL), jnp.float32),
                        pltpu.SemaphoreType.DMA(())],
        compiler_params=_params(("arbitrary",)),
        name="combine_ln",
    )(dest_flat, x, gates, g.reshape(1, D_MODEL), b.reshape(1, D_MODEL), y)


def _rope_tables(seq):
    inv_freq = ROPE_THETA ** (-jnp.arange(0, ROPE_DIM, 2, dtype=jnp.float32) / ROPE_DIM)
    ang = jnp.arange(seq, dtype=jnp.float32)[:, None] * inv_freq[None, :]
    cos, sin = jnp.cos(ang), jnp.sin(ang)
    pad = HEAD_DIM - ROPE_DIM
    cos_full = jnp.concatenate([cos, cos, jnp.ones((seq, pad), jnp.float32)], axis=1)
    sin_full = jnp.concatenate([-sin, sin, jnp.zeros((seq, pad), jnp.float32)], axis=1)
    return cos_full, sin_full


def _mixer(x, x_bf, batch, w_in, conv_w, w_br_conv, w_br_attn, pool_mix, pool_scale, w_br_pool, w_o,
           ln_g, ln_b, rope_cos, rope_sin):
    conv_cols = CONV_KERNEL * CONV_WIDTH
    pool_col = conv_cols + QKV_WIDTH
    conv_p = _project(x_bf, w_in, 0, conv_cols, 1024, jnp.bfloat16)
    qkv = _project(x_bf, w_in, conv_cols, QKV_WIDTH, 1536, jnp.bfloat16)
    w_gates_pool = jnp.concatenate([w_in[:, pool_col + POOL_WIDTH:], w_in[:, pool_col:pool_col + POOL_WIDTH]], axis=1)
    gates_pool_p = _project(x_bf, w_gates_pool, 0, N_GROUPS * D_MODEL + POOL_WIDTH, 1024, jnp.bfloat16)
    attn = _attention(qkv.reshape(batch, SEQ, QKV_WIDTH), rope_cos, rope_sin)
    attn = attn.reshape(batch * SEQ, GROUP_WIDTH)
    bf = lambda w: w.astype(jnp.bfloat16)
    merged = _branches(conv_p, gates_pool_p, attn, conv_w, pool_scale,
                       bf(w_br_conv), bf(w_br_attn), bf(w_br_pool), bf(pool_mix))
    return _out_proj_ln(merged, bf(w_o), x, ln_g, ln_b)


def _dense_ffn(x, x_bf, w_gate, w_up, w_down, ln_g, ln_b, tm=1024, tf=512):
    n_blk = x.shape[0] // tm
    blk_exp = jnp.zeros((n_blk,), jnp.int32)
    blk_rows = jnp.full((n_blk,), tm, jnp.int32)
    n_used = jnp.full((1,), n_blk, jnp.int32)
    f = _swiglu_blocks(x_bf, w_gate[None], w_up[None], w_down[None], blk_exp, blk_rows, n_used, tm, tf)
    return _resid_ln(x, f, ln_g, ln_b)


def _moe_ffn(x, w_router, w_gate, w_up, w_down, ln_g, ln_b, tm=1024, tf=512):
    m = x.shape[0]
    idx, gates, rank, cnt = _route(x, w_router)
    counts = cnt[0, :N_EXPERTS].astype(jnp.int32)
    padded = (counts + tm - 1) // tm * tm
    pend = jnp.cumsum(padded)
    pstart = pend - padded
    dest_flat = (pstart[idx] + rank).reshape(-1)
    n_blk = m * TOP_K // tm + N_EXPERTS
    blk_start = jnp.arange(n_blk, dtype=jnp.int32) * tm
    blk_exp = jnp.minimum(jnp.sum(blk_start[:, None] >= pend[None, :], axis=1), N_EXPERTS - 1).astype(jnp.int32)
    blk_rows = jnp.clip(pstart[blk_exp] + counts[blk_exp] - blk_start, 0, tm).astype(jnp.int32)
    n_used = (pend[-1:] // tm).astype(jnp.int32)
    blk_rows = jnp.where(jnp.arange(n_blk) < n_used[0], blk_rows, 0)
    xs = _dispatch(x, dest_flat, n_blk * tm)
    ys = _swiglu_blocks(xs, w_gate, w_up, w_down, blk_exp, blk_rows, n_used, tm, tf)
    return _combine_ln(x, ys, dest_flat, gates, ln_g, ln_b)


def kernel(x, l0_w_in, l0_conv_w, l0_w_br_conv, l0_w_br_attn, l0_pool_mix, l0_pool_scale, l0_w_br_pool, l0_w_o, l0_ln1_g, l0_ln1_b, l0_ffn_gate, l0_ffn_up, l0_ffn_down, l0_ln2_g, l0_ln2_b, l1_w_in, l1_conv_w, l1_w_br_conv, l1_w_br_attn, l1_pool_mix, l1_pool_scale, l1_w_br_pool, l1_w_o, l1_ln1_g, l1_ln1_b, l1_router, l1_exp_gate, l1_exp_up, l1_exp_down, l1_ln2_g, l1_ln2_b):
    batch, seq, d = x.shape
    assert seq == SEQ and d == D_MODEL
    rope_cos, rope_sin = _rope_tables(seq)
    x0 = x.reshape(batch * seq, d)
    x1, x1_bf = _mixer(x0, x0.astype(jnp.bfloat16), batch, l0_w_in, l0_conv_w, l0_w_br_conv, l0_w_br_attn,
                       l0_pool_mix, l0_pool_scale, l0_w_br_pool, l0_w_o, l0_ln1_g, l0_ln1_b, rope_cos, rope_sin)
    x2, x2_bf = _dense_ffn(x1, x1_bf, l0_ffn_gate, l0_ffn_up, l0_ffn_down, l0_ln2_g, l0_ln2_b)
    x3, _ = _mixer(x2, x2_bf, batch, l1_w_in, l1_conv_w, l1_w_br_conv, l1_w_br_attn,
                   l1_pool_mix, l1_pool_scale, l1_w_br_pool, l1_w_o, l1_ln1_g, l1_ln1_b, rope_cos, rope_sin)
    x4 = _moe_ffn(x3, l1_router, l1_exp_gate, l1_exp_up, l1_exp_down, l1_ln2_g, l1_ln2_b)
    return x4.reshape(batch, seq, d)
```

```python
import functools

import jax
import jax.numpy as jnp
from jax import lax
from jax.experimental import pallas as pl
from jax.experimental.pallas import tpu as pltpu

D_MODEL = 2048
SEQ = 4096
CONV_WIDTH = D_MODEL // 2
CONV_KERNEL = 3
HEAD_DIM = 128
HEADS_PER_GROUP = 4
GROUP_WIDTH = HEADS_PER_GROUP * HEAD_DIM
ATTN_PATTERNS = ((128, 1), (512, 4), (2048, 16))
N_GROUPS = len(ATTN_PATTERNS)
ATTN_HEADS = N_GROUPS * HEADS_PER_GROUP
QKV_WIDTH = 3 * N_GROUPS * GROUP_WIDTH
ATTN_BLOCK = 128
ROPE_THETA = 500000.0
ROPE_DIM = HEAD_DIM // 4
POOL_WINDOWS = (2, 4, 8, 16)
POOL_WIDTH = D_MODEL // 2
POOL_GROUP_DIM = POOL_WIDTH // len(POOL_WINDOWS)
POOL_HALO = 16
CONV_HALO = 16
D_FF = 7 * D_MODEL // 2
N_EXPERTS = 8
TOP_K = 2
LN_EPS = 1e-5
DEPTH = 2
DEEPNORM_ALPHA = (2.0 * DEPTH) ** 0.25
NEG_INF = -1e30
LANES = 128

VMEM_LIMIT = 56 * 1024 * 1024


def _params(sem, vmem=VMEM_LIMIT):
    return pltpu.CompilerParams(dimension_semantics=sem, vmem_limit_bytes=vmem)


W_SLAB = 512


def _rope(t, cos, sin):
    half = ROPE_DIM // 2
    lane = lax.broadcasted_iota(jnp.int32, t.shape, 1)
    partner = jnp.where(lane < half, pltpu.roll(t, HEAD_DIM - half, 1), pltpu.roll(t, half, 1))
    return t * cos + partner * sin


def _proj_kernel(*refs, n_slabs, rope_tiles):
    x_ref, w_refs = refs[0], refs[1:1 + n_slabs]
    o_ref, wbf_ref = refs[-2], refs[-1]

    @pl.when(pl.program_id(1) == 0)
    def _():
        for t, w_ref in enumerate(w_refs):
            wbf_ref[:, t * W_SLAB:(t + 1) * W_SLAB] = w_ref[...].astype(jnp.bfloat16)

    def plain():
        o_ref[...] = jnp.dot(x_ref[...], wbf_ref[...],
                             preferred_element_type=jnp.float32).astype(o_ref.dtype)

    if rope_tiles == 0:
        plain()
        return
    cos_ref, sin_ref = refs[1 + n_slabs], refs[2 + n_slabs]
    is_rope_tile = pl.program_id(0) < rope_tiles

    @pl.when(is_rope_tile)
    def _():
        cos, sin = cos_ref[...], sin_ref[...]
        for t in range(n_slabs):
            acc = jnp.dot(x_ref[...], wbf_ref[:, t * W_SLAB:(t + 1) * W_SLAB],
                          preferred_element_type=jnp.float32)
            for h in range(W_SLAB // HEAD_DIM):
                cols = slice(h * HEAD_DIM, (h + 1) * HEAD_DIM)
                out_cols = slice(t * W_SLAB + h * HEAD_DIM, t * W_SLAB + (h + 1) * HEAD_DIM)
                o_ref[:, out_cols] = _rope(acc[:, cols], cos, sin).astype(o_ref.dtype)

    pl.when(jnp.logical_not(is_rope_tile))(plain)


def _project(x_bf, w, col0, width, tn, out_dtype, tm=1024, rope=None, rope_tiles=0):
    m, k = x_bf.shape
    assert col0 % W_SLAB == 0 and tn % W_SLAB == 0 and width % tn == 0 and m % tm == 0
    n_slabs = tn // W_SLAB
    slab0 = col0 // W_SLAB
    w_specs = [pl.BlockSpec((k, W_SLAB), lambda j, i, t=t: (0, slab0 + j * n_slabs + t))
               for t in range(n_slabs)]
    rope_specs, rope_args = [], []
    if rope_tiles:
        tiles_per_seq = SEQ // tm
        rope_specs = [pl.BlockSpec((tm, LANES), lambda j, i: (i % tiles_per_seq, 0))] * 2
        rope_args = list(rope)
    return pl.pallas_call(
        functools.partial(_proj_kernel, n_slabs=n_slabs, rope_tiles=rope_tiles),
        grid=(width // tn, m // tm),
        in_specs=[pl.BlockSpec((tm, k), lambda j, i: (i, 0))] + w_specs + rope_specs,
        out_specs=pl.BlockSpec((tm, tn), lambda j, i: (i, j)),
        out_shape=jax.ShapeDtypeStruct((m, width), out_dtype),
        scratch_shapes=[pltpu.VMEM((k, tn), jnp.bfloat16)],
        compiler_params=_params(("arbitrary", "arbitrary")),
        name="in_proj",
    )(x_bf, *([w] * n_slabs), *rope_args)


def _strided(start, size, stride):
    return pl.ds(start, size) if stride == 1 else pl.ds(start, size, stride=stride)


def _attn_kernel(q0_ref, k0_ref, v0_ref, q1_ref, k1_ref, v1_ref, q2_ref, k2_ref, v2_ref,
                 o_ref, q_s, k_s, v_s, out_s, lse_s, *, chunk, chains):
    groups = ((q0_ref, k0_ref, v0_ref), (q1_ref, k1_ref, v1_ref), (q2_ref, k2_ref, v2_ref))
    n_keys = 2 * ATTN_BLOCK
    a = lax.broadcasted_iota(jnp.int32, (ATTN_BLOCK, n_keys), 0)
    kk = lax.broadcasted_iota(jnp.int32, (ATTN_BLOCK, n_keys), 1)
    scale = HEAD_DIM ** -0.5

    for g, (q_ref, k_ref, v_ref) in enumerate(groups):
        d = ATTN_PATTERNS[g][1]
        if d > 1:
            def stage(ci, carry, q_ref=q_ref, k_ref=k_ref, v_ref=v_ref):
                rows = pl.ds(pl.multiple_of(ci * chunk, chunk), chunk)
                q_s[rows, :] = q_ref[0, rows, :].astype(jnp.float32)
                k_s[rows, :] = k_ref[0, rows, :].astype(jnp.float32)
                v_s[rows, :] = v_ref[0, rows, :].astype(jnp.float32)
                return carry

            lax.fori_loop(0, SEQ // chunk, stage, 0)
            q_src, k_src, v_src = q_s, k_s, v_s
        else:
            q_src, k_src, v_src = q_ref.at[0], k_ref.at[0], v_ref.at[0]

        blocks_per_residue = SEQ // d // ATTN_BLOCK
        unroll = min(chains, blocks_per_residue)
        residues_per_body = chains // unroll
        for c0 in range(0, d, residues_per_body):
            def block(blk, carry, c0=c0, d=d, g=g, residues_per_body=residues_per_body,
                      q_src=q_src, k_src=k_src, v_src=v_src):
                base = pl.multiple_of(blk * (ATTN_BLOCK * d), ATTN_BLOCK)
                key_base = pl.multiple_of(jnp.maximum(base - ATTN_BLOCK * d, 0), ATTN_BLOCK)
                self_key = jnp.where(blk == 0, 0, ATTN_BLOCK)
                valid = (kk >= a + self_key - ATTN_BLOCK) & (kk <= a + self_key)
                for c in range(c0, c0 + residues_per_body):
                    qb = q_src[_strided(base + c, ATTN_BLOCK, d), :].astype(jnp.bfloat16)
                    kb = k_src[_strided(key_base + c, n_keys, d), :].astype(jnp.bfloat16)
                    vb = v_src[_strided(key_base + c, n_keys, d), :].astype(jnp.bfloat16)
                    s = lax.dot_general(qb, kb, (((1,), (1,)), ((), ())),
                                        preferred_element_type=jnp.float32) * scale
                    s = jnp.where(valid, s, NEG_INF)
                    m = jnp.max(s, axis=-1, keepdims=True)
                    p = jnp.exp(s - m)
                    l = jnp.sum(p, axis=-1, keepdims=True)
                    o = jnp.dot(p.astype(jnp.bfloat16), vb, preferred_element_type=jnp.float32)
                    rows = _strided(base + c, ATTN_BLOCK, d)
                    out_s[g, rows, :] = o / l
                    lse_s[g, rows, :] = jnp.broadcast_to(m + jnp.log(l), (ATTN_BLOCK, HEAD_DIM))
                return carry

            lax.fori_loop(0, blocks_per_residue, block, 0, unroll=unroll)

    def combine(ci, carry):
        rows = pl.ds(pl.multiple_of(ci * chunk, chunk), chunk)
        l0, l1, l2 = lse_s[0, rows, :], lse_s[1, rows, :], lse_s[2, rows, :]
        mx = jnp.maximum(jnp.maximum(l0, l1), l2)
        e0, e1, e2 = jnp.exp(l0 - mx), jnp.exp(l1 - mx), jnp.exp(l2 - mx)
        num = e0 * out_s[0, rows, :] + e1 * out_s[1, rows, :] + e2 * out_s[2, rows, :]
        o_ref[0, rows, :] = (num / (e0 + e1 + e2)).astype(o_ref.dtype)
        return carry

    lax.fori_loop(0, SEQ // chunk, combine, 0)


def _attention(qkv, chunk=256, chains=4):
    b, s, _ = qkv.shape
    assert s == SEQ

    def head_block(part, group):
        col0 = part * ATTN_HEADS + group * HEADS_PER_GROUP
        return pl.BlockSpec((1, s, HEAD_DIM), lambda bi, h: (bi, 0, col0 + h))

    in_specs = [head_block(part, group) for group in range(N_GROUPS) for part in range(3)]
    return pl.pallas_call(
        functools.partial(_attn_kernel, chunk=chunk, chains=chains),
        grid=(b, HEADS_PER_GROUP),
        in_specs=in_specs,
        out_specs=pl.BlockSpec((1, s, HEAD_DIM), lambda bi, h: (bi, 0, h)),
        out_shape=jax.ShapeDtypeStruct((b, s, GROUP_WIDTH), jnp.bfloat16),
        scratch_shapes=[pltpu.VMEM((s, HEAD_DIM), jnp.float32),
                        pltpu.VMEM((s, HEAD_DIM), jnp.float32),
                        pltpu.VMEM((s, HEAD_DIM), jnp.float32),
                        pltpu.VMEM((N_GROUPS, s, HEAD_DIM), jnp.float32),
                        pltpu.VMEM((N_GROUPS, s, HEAD_DIM), jnp.float32)],
        compiler_params=_params(("arbitrary", "arbitrary")),
        name="dilated_attn",
    )(*([qkv] * 9))


def _branch_kernel(cb_ref, cc_ref, ch_ref, cch_ref, chh_ref, pu_ref, puh_ref, attn_ref,
                   g0a_ref, g0b_ref, g1a_ref, g1b_ref, g2a_ref, g2b_ref, convw_ref, pscale_ref,
                   wconv_ref, wattn_ref, wpool_ref, pmix_ref, o_ref, *, tm, tiles_per_seq):
    seq_tile = pl.program_id(0) % tiles_per_seq
    not_first = (seq_tile != 0).astype(jnp.float32)
    f32 = lambda ref: ref[...].astype(jnp.float32)
    gate = lambda lo_ref, hi_ref: jax.nn.sigmoid(jnp.concatenate([f32(lo_ref), f32(hi_ref)], axis=1))

    u = f32(cc_ref) * f32(ch_ref)
    uh = f32(cch_ref) * f32(chh_ref) * not_first
    ext = jnp.concatenate([uh, u], axis=0)
    u1 = pltpu.roll(ext, 1, 0)[CONV_HALO:]
    u2 = pltpu.roll(ext, 2, 0)[CONV_HALO:]
    cw = convw_ref[...]
    conv = u2 * cw[0:1] + u1 * cw[1:2]
    conv = conv + u * cw[2:3]
    ya_in = (f32(cb_ref) * conv).astype(jnp.bfloat16)
    y_a = jnp.dot(ya_in, wconv_ref[...], preferred_element_type=jnp.float32)
    merged = gate(g0a_ref, g0b_ref) * y_a

    y_b = jnp.dot(attn_ref[...], wattn_ref[...], preferred_element_type=jnp.float32)
    merged = merged + gate(g1a_ref, g1b_ref) * y_b

    pu = f32(pu_ref)
    pext = jnp.concatenate([f32(puh_ref) * not_first, pu], axis=0)
    pos = seq_tile * tm + lax.broadcasted_iota(jnp.int32, (tm, 1), 0) + 1
    pscale = pscale_ref[...]
    y_c = None
    for gi, w in enumerate(POOL_WINDOWS):
        cols = slice(gi * POOL_GROUP_DIM, (gi + 1) * POOL_GROUP_DIM)
        run = pext[:, cols]
        step = 1
        while step < w:
            run = run + pltpu.roll(run, step, 0)
            step *= 2
        inv_cnt = 1.0 / jnp.minimum(pos, w).astype(jnp.float32)
        pooled = run[POOL_HALO:] * inv_cnt - pu[:, cols]
        mixed = jnp.dot(pooled.astype(jnp.bfloat16), pmix_ref[gi], preferred_element_type=jnp.float32)
        mixed = (mixed * pscale[:, cols]).astype(jnp.bfloat16)
        part = jnp.dot(mixed, wpool_ref[cols, :], preferred_element_type=jnp.float32)
        y_c = part if y_c is None else y_c + part
    merged = merged + gate(g2a_ref, g2b_ref) * y_c
    o_ref[...] = merged.astype(o_ref.dtype)


def _branches(conv_p, pool_gates_p, attn, conv_w, pool_scale,
              w_conv_bf, w_attn_bf, w_pool_bf, pool_mix_bf, tm=256):
    m = conv_p.shape[0]
    gate_blocks = N_GROUPS * D_MODEL // CONV_WIDTH
    tiles_per_seq = SEQ // tm
    row = lambda width, col: pl.BlockSpec((tm, width), lambda i: (i, col))

    def halo(rows, col):
        per_tile = tm // rows
        return pl.BlockSpec((rows, CONV_WIDTH), lambda i: (jnp.maximum(i * per_tile - 1, 0), col))

    full = lambda shape: pl.BlockSpec(shape, lambda i: (0,) * len(shape), pipeline_mode=pl.Buffered(1))
    in_specs = [row(CONV_WIDTH, 0), row(CONV_WIDTH, 1), row(CONV_WIDTH, 2),
                halo(CONV_HALO, 1), halo(CONV_HALO, 2),
                row(POOL_WIDTH, 0), halo(POOL_HALO, 0),
                row(GROUP_WIDTH, 0)]
    in_specs += [row(CONV_WIDTH, 1 + gb) for gb in range(gate_blocks)]
    in_specs += [full((CONV_KERNEL, CONV_WIDTH)), full((1, POOL_WIDTH)),
                 full((CONV_WIDTH, D_MODEL)), full((GROUP_WIDTH, D_MODEL)), full((POOL_WIDTH, D_MODEL)),
                 full((len(POOL_WINDOWS), POOL_GROUP_DIM, POOL_GROUP_DIM))]
    return pl.pallas_call(
        functools.partial(_branch_kernel, tm=tm, tiles_per_seq=tiles_per_seq),
        grid=(m // tm,),
        in_specs=in_specs,
        out_specs=pl.BlockSpec((tm, D_MODEL), lambda i: (i, 0)),
        out_shape=jax.ShapeDtypeStruct((m, D_MODEL), jnp.bfloat16),
        compiler_params=_params(("arbitrary",)),
        name="branches",
    )(conv_p, conv_p, conv_p, conv_p, conv_p, pool_gates_p, pool_gates_p, attn,
      *([pool_gates_p] * gate_blocks), conv_w, pool_scale.reshape(1, POOL_WIDTH),
      w_conv_bf, w_attn_bf, w_pool_bf, pool_mix_bf)


def _resid_layer_norm(x, f, g, b):
    z = DEEPNORM_ALPHA * x + f
    mu = jnp.mean(z, axis=-1, keepdims=True)
    zc = z - mu
    var = jnp.mean(zc * zc, axis=-1, keepdims=True)
    return zc * lax.rsqrt(var + LN_EPS) * g + b


def _route_tile(y, w_ref, idx_ref, gate_ref, rank_ref, cnt_ref, carry_ref, *, tm):
    @pl.when(pl.program_id(0) == 0)
    def _():
        carry_ref[...] = jnp.zeros_like(carry_ref)

    w = w_ref[...]
    y_hi, w_hi = y.astype(jnp.bfloat16), w.astype(jnp.bfloat16)
    y_lo = (y - y_hi.astype(jnp.float32)).astype(jnp.bfloat16)
    w_lo = (w - w_hi.astype(jnp.float32)).astype(jnp.bfloat16)
    logits = (jnp.dot(y_hi, w_hi, preferred_element_type=jnp.float32)
              + jnp.dot(y_lo, w_hi, preferred_element_type=jnp.float32)
              + jnp.dot(y_hi, w_lo, preferred_element_type=jnp.float32))
    lane = lax.broadcasted_iota(jnp.int32, logits.shape, 1)
    logits = jnp.where(lane < N_EXPERTS, logits, -jnp.inf)
    m1 = jnp.max(logits, axis=-1, keepdims=True)
    i1 = jnp.min(jnp.where(logits == m1, lane, LANES), axis=-1, keepdims=True)
    rest = jnp.where(lane == i1, -jnp.inf, logits)
    m2 = jnp.max(rest, axis=-1, keepdims=True)
    i2 = jnp.min(jnp.where(rest == m2, lane, LANES), axis=-1, keepdims=True)
    e = jnp.exp(m2 - m1)
    den = 1.0 + e
    idx_ref[:, 0:1] = i1
    idx_ref[:, 1:2] = i2
    gate_ref[:, 0:1] = 1.0 / den
    gate_ref[:, 1:2] = e / den

    oh1 = (lane == i1).astype(jnp.float32)
    oh2 = (lane == i2).astype(jnp.float32)
    both = oh1 + oh2
    r = lax.broadcasted_iota(jnp.int32, (tm, tm), 0)
    c = lax.broadcasted_iota(jnp.int32, (tm, tm), 1)
    strict_lower = (r > c).astype(jnp.bfloat16)
    before = jnp.dot(strict_lower, both.astype(jnp.bfloat16), preferred_element_type=jnp.float32)
    before = before + carry_ref[...]
    rank_ref[:, 0:1] = jnp.sum(oh1 * before, axis=-1, keepdims=True).astype(jnp.int32)
    rank_ref[:, 1:2] = jnp.sum(oh2 * before, axis=-1, keepdims=True).astype(jnp.int32)
    carry_ref[...] += jnp.sum(both, axis=0, keepdims=True)
    cnt_ref[...] = carry_ref[...]


def _route_kernel(x_ref, w_ref, idx_ref, gate_ref, rank_ref, cnt_ref, carry_ref, *, tm):
    _route_tile(x_ref[...], w_ref, idx_ref, gate_ref, rank_ref, cnt_ref, carry_ref, tm=tm)


def _route(x, w_router, tm=512):
    m = x.shape[0]
    w_pad = jnp.pad(w_router, ((0, 0), (0, LANES - N_EXPERTS)))
    pair = pl.BlockSpec((tm, TOP_K), lambda i: (i, 0))
    return pl.pallas_call(
        functools.partial(_route_kernel, tm=tm),
        grid=(m // tm,),
        in_specs=[pl.BlockSpec((tm, D_MODEL), lambda i: (i, 0)),
                  pl.BlockSpec((D_MODEL, LANES), lambda i: (0, 0))],
        out_specs=[pair, pair, pair, pl.BlockSpec((1, LANES), lambda i: (0, 0))],
        out_shape=[jax.ShapeDtypeStruct((m, TOP_K), jnp.int32),
                   jax.ShapeDtypeStruct((m, TOP_K), jnp.float32),
                   jax.ShapeDtypeStruct((m, TOP_K), jnp.int32),
                   jax.ShapeDtypeStruct((1, LANES), jnp.float32)],
        scratch_shapes=[pltpu.VMEM((1, LANES), jnp.float32)],
        compiler_params=_params(("arbitrary",)),
        name="router",
    )(x, w_pad)


def _out_proj_ln_kernel(m_ref, w_ref, x_ref, g_ref, b_ref, o_ref, obf_ref):
    f = jnp.dot(m_ref[...], w_ref[...], preferred_element_type=jnp.float32)
    y = _resid_layer_norm(x_ref[...], f, g_ref[...], b_ref[...])
    o_ref[...] = y
    obf_ref[...] = y.astype(jnp.bfloat16)


def _out_proj_ln(merged_bf, w_o_bf, x, g, b, tm=512):
    m = x.shape[0]
    row = pl.BlockSpec((tm, D_MODEL), lambda i: (i, 0))
    vec = pl.BlockSpec((1, D_MODEL), lambda i: (0, 0))
    return pl.pallas_call(
        _out_proj_ln_kernel,
        grid=(m // tm,),
        in_specs=[row, pl.BlockSpec((D_MODEL, D_MODEL), lambda i: (0, 0)), row, vec, vec],
        out_specs=[row, row],
        out_shape=[jax.ShapeDtypeStruct((m, D_MODEL), jnp.float32),
                   jax.ShapeDtypeStruct((m, D_MODEL), jnp.bfloat16)],
        compiler_params=_params(("arbitrary",)),
        name="out_proj_ln",
    )(merged_bf, w_o_bf, x, g.reshape(1, D_MODEL), b.reshape(1, D_MODEL))


def _swiglu_kernel(blk_exp_ref, blk_rows_ref, n_used_ref, x_ref, wg_ref, wu_ref, wd_ref, *refs,
                   tm, sub, last_j, resid_ln):
    del blk_exp_ref, n_used_ref
    if resid_ln:
        res_ref, g_ref, b_ref, o_ref, obf_ref = refs[:5]
        scratch = refs[5:]
    else:
        o_ref, scratch = refs[0], refs[1:]
    xbf_ref = scratch[0] if scratch else x_ref
    i, j = pl.program_id(0), pl.program_id(1)
    n_sub = (blk_rows_ref[i] + sub - 1) // sub

    @pl.when(j == 0)
    def _():
        if scratch:
            xbf_ref[...] = x_ref[...].astype(jnp.bfloat16)
        o_ref[...] = jnp.zeros_like(o_ref)

    for v in range(1, tm // sub + 1):
        rows = v * sub

        @pl.when(n_sub == v)
        def _(rows=rows):
            x = xbf_ref[0:rows, :]
            gate = jnp.dot(x, wg_ref[0].astype(jnp.bfloat16), preferred_element_type=jnp.float32)
            up = jnp.dot(x, wu_ref[0].astype(jnp.bfloat16), preferred_element_type=jnp.float32)
            h = (gate * jax.nn.sigmoid(gate) * up).astype(jnp.bfloat16)
            o_ref[0:rows, :] += jnp.dot(h, wd_ref[0].astype(jnp.bfloat16),
                                        preferred_element_type=jnp.float32)

    if resid_ln:
        @pl.when(j == last_j)
        def _():
            def norm_rows(ci, carry):
                rows = pl.ds(pl.multiple_of(ci * sub, sub), sub)
                y = _resid_layer_norm(res_ref[rows, :], o_ref[rows, :], g_ref[...], b_ref[...])
                o_ref[rows, :] = y
                obf_ref[rows, :] = y.astype(jnp.bfloat16)
                return carry

            lax.fori_loop(0, tm // sub, norm_rows, 0)


def _swiglu_blocks(x, w_gate, w_up, w_down, blk_exp, blk_rows, n_used, tm, tf, sub=256, resid_ln=None):
    r = x.shape[0]
    nf = D_FF // tf
    last = nf - 1

    def live(i, nu):
        return jnp.maximum(jnp.minimum(i, nu[0] - 1), 0)

    def jj(i, j, nu):
        return jnp.where(i < nu[0], j, last)

    once = pl.Buffered(1)
    row_in = pl.BlockSpec((tm, D_MODEL), lambda i, j, be, br, nu: (live(i, nu), 0), pipeline_mode=once)
    row_out = pl.BlockSpec((tm, D_MODEL), lambda i, j, be, br, nu: (i, 0), pipeline_mode=once)
    vec = pl.BlockSpec((1, D_MODEL), lambda i, j, be, br, nu: (0, 0))
    in_specs = [row_in,
                pl.BlockSpec((1, D_MODEL, tf), lambda i, j, be, br, nu: (be[live(i, nu)], 0, jj(i, j, nu))),
                pl.BlockSpec((1, D_MODEL, tf), lambda i, j, be, br, nu: (be[live(i, nu)], 0, jj(i, j, nu))),
                pl.BlockSpec((1, tf, D_MODEL), lambda i, j, be, br, nu: (be[live(i, nu)], jj(i, j, nu), 0))]
    args = [x, w_gate, w_up, w_down]
    out_specs = row_out
    out_shape = jax.ShapeDtypeStruct((r, D_MODEL), jnp.float32)
    if resid_ln is not None:
        res, g, b = resid_ln
        in_specs += [row_in, vec, vec]
        args += [res, g.reshape(1, D_MODEL), b.reshape(1, D_MODEL)]
        out_specs = [row_out, row_out]
        out_shape = [out_shape, jax.ShapeDtypeStruct((r, D_MODEL), jnp.bfloat16)]
    scratch = [] if x.dtype == jnp.bfloat16 else [pltpu.VMEM((tm, D_MODEL), jnp.bfloat16)]
    grid_spec = pltpu.PrefetchScalarGridSpec(
        num_scalar_prefetch=3, grid=(r // tm, nf),
        in_specs=in_specs, out_specs=out_specs, scratch_shapes=scratch)
    return pl.pallas_call(
        functools.partial(_swiglu_kernel, tm=tm, sub=sub, last_j=last, resid_ln=resid_ln is not None),
        grid_spec=grid_spec,
        out_shape=out_shape,
        compiler_params=_params(("arbitrary", "arbitrary")),
        name="swiglu_blocks",
    )(blk_exp, blk_rows, n_used, *args)


def _dispatch_kernel(dest_ref, x_ref, xs_in_ref, xs_ref, sem, *, tm):
    del xs_in_ref

    def issue(t, carry):
        for k in range(TOP_K):
            pltpu.make_async_copy(x_ref.at[pl.ds(t, 1)],
                                  xs_ref.at[pl.ds(dest_ref[TOP_K * t + k], 1)], sem).start()
        return carry

    lax.fori_loop(0, tm, issue, 0, unroll=8)
    for k in range(TOP_K):
        pltpu.make_async_copy(x_ref, xs_ref.at[pl.ds(0, tm)], sem).wait()


def _dispatch(x, dest_flat, rows, tm=256):
    m = x.shape[0]
    zeros = jnp.zeros((rows, D_MODEL), x.dtype)
    return pl.pallas_call(
        functools.partial(_dispatch_kernel, tm=tm),
        grid=(m // tm,),
        in_specs=[pl.BlockSpec((TOP_K * tm,), lambda i: (i,), memory_space=pltpu.SMEM),
                  pl.BlockSpec((tm, D_MODEL), lambda i: (i, 0)),
                  pl.BlockSpec(memory_space=pl.ANY)],
        out_specs=pl.BlockSpec(memory_space=pl.ANY),
        out_shape=jax.ShapeDtypeStruct((rows, D_MODEL), x.dtype),
        scratch_shapes=[pltpu.SemaphoreType.DMA(())],
        input_output_aliases={2: 0},
        compiler_params=_params(("arbitrary",)),
        name="dispatch",
    )(dest_flat, x, zeros)


def _combine_ln_kernel(dest_ref, x_ref, gate_ref, g_ref, b_ref, y_ref, o_ref, ybuf_ref, sem, *, tm):
    def issue(t, carry):
        for k in range(TOP_K):
            pltpu.make_async_copy(y_ref.at[pl.ds(dest_ref[TOP_K * t + k], 1)],
                                  ybuf_ref.at[k, pl.ds(t, 1)], sem).start()
        return carry

    lax.fori_loop(0, tm, issue, 0, unroll=8)
    for k in range(TOP_K):
        pltpu.make_async_copy(y_ref.at[pl.ds(0, tm)], ybuf_ref.at[k], sem).wait()
    gate = gate_ref[...]
    f = ybuf_ref[0] * gate[:, 0:1] + ybuf_ref[1] * gate[:, 1:2]
    o_ref[...] = _resid_layer_norm(x_ref[...], f, g_ref[...], b_ref[...])


def _combine_ln(x, y, dest_flat, gates, g, b, tm=256):
    m = x.shape[0]
    row = pl.BlockSpec((tm, D_MODEL), lambda i: (i, 0))
    vec = pl.BlockSpec((1, D_MODEL), lambda i: (0, 0))
    return pl.pallas_call(
        functools.partial(_combine_ln_kernel, tm=tm),
        grid=(m // tm,),
        in_specs=[pl.BlockSpec((TOP_K * tm,), lambda i: (i,), memory_space=pltpu.SMEM),
                  row, pl.BlockSpec((tm, TOP_K), lambda i: (i, 0)), vec, vec,
                  pl.BlockSpec(memory_space=pl.ANY)],
        out_specs=row,
        out_shape=jax.ShapeDtypeStruct((m, D_MODEL), jnp.float32),
        scratch_shapes=[pltpu.VMEM((TOP_K, tm, D_MODEL), jnp.float32),
                        pltpu.SemaphoreType.DMA(())],
        compiler_params=_params(("arbitrary",)),
        name="combine_ln",
    )(dest_flat, x, gates, g.reshape(1, D_MODEL), b.reshape(1, D_MODEL), y)


def _rope_tables(seq):
    inv_freq = ROPE_THETA ** (-jnp.arange(0, ROPE_DIM, 2, dtype=jnp.float32) / ROPE_DIM)
    ang = jnp.arange(seq, dtype=jnp.float32)[:, None] * inv_freq[None, :]
    cos, sin = jnp.cos(ang), jnp.sin(ang)
    pad = HEAD_DIM - ROPE_DIM
    cos_full = jnp.concatenate([cos, cos, jnp.ones((seq, pad), jnp.float32)], axis=1)
    sin_full = jnp.concatenate([-sin, sin, jnp.zeros((seq, pad), jnp.float32)], axis=1)
    return cos_full, sin_full


def _mixer(x, x_bf, batch, w_in, conv_w, w_br_conv, w_br_attn, pool_mix, pool_scale, w_br_pool, w_o,
           ln_g, ln_b, rope_cos, rope_sin):
    conv_cols = CONV_KERNEL * CONV_WIDTH
    pool_col = conv_cols + QKV_WIDTH
    conv_p = _project(x_bf, w_in, 0, conv_cols, 1024, jnp.bfloat16)
    qkv = _project(x_bf, w_in, conv_cols, QKV_WIDTH, QKV_WIDTH // 3, jnp.bfloat16,
                   rope=(rope_cos, rope_sin), rope_tiles=2)
    pool_gates_p = _project(x_bf, w_in, pool_col, POOL_WIDTH + N_GROUPS * D_MODEL, 1024, jnp.bfloat16)
    attn = _attention(qkv.reshape(batch, SEQ, QKV_WIDTH))
    attn = attn.reshape(batch * SEQ, GROUP_WIDTH)
    bf = lambda w: w.astype(jnp.bfloat16)
    merged = _branches(conv_p, pool_gates_p, attn, conv_w, pool_scale,
                       bf(w_br_conv), bf(w_br_attn), bf(w_br_pool), bf(pool_mix))
    return _out_proj_ln(merged, bf(w_o), x, ln_g, ln_b)


def _dense_ffn(x, x_bf, w_gate, w_up, w_down, ln_g, ln_b, tm=1024, tf=512):
    n_blk = x.shape[0] // tm
    blk_exp = jnp.zeros((n_blk,), jnp.int32)
    blk_rows = jnp.full((n_blk,), tm, jnp.int32)
    n_used = jnp.full((1,), n_blk, jnp.int32)
    return _swiglu_blocks(x_bf, w_gate[None], w_up[None], w_down[None], blk_exp, blk_rows, n_used, tm, tf,
                          resid_ln=(x, ln_g, ln_b))


def _moe_ffn(x, w_router, w_gate, w_up, w_down, ln_g, ln_b, tm=1024, tf=512):
    m = x.shape[0]
    idx, gates, rank, cnt = _route(x, w_router)
    counts = cnt[0, :N_EXPERTS].astype(jnp.int32)
    padded = (counts + tm - 1) // tm * tm
    pend = jnp.cumsum(padded)
    pstart = pend - padded
    dest_flat = (pstart[idx] + rank).reshape(-1)
    n_blk = m * TOP_K // tm + N_EXPERTS
    blk_start = jnp.arange(n_blk, dtype=jnp.int32) * tm
    blk_exp = jnp.minimum(jnp.sum(blk_start[:, None] >= pend[None, :], axis=1), N_EXPERTS - 1).astype(jnp.int32)
    blk_rows = jnp.clip(pstart[blk_exp] + counts[blk_exp] - blk_start, 0, tm).astype(jnp.int32)
    n_used = (pend[-1:] // tm).astype(jnp.int32)
    blk_rows = jnp.where(jnp.arange(n_blk) < n_used[0], blk_rows, 0)
    xs = _dispatch(x, dest_flat, n_blk * tm)
    ys = _swiglu_blocks(xs, w_gate, w_up, w_down, blk_exp, blk_rows, n_used, tm, tf)
    return _combine_ln(x, ys, dest_flat, gates, ln_g, ln_b)


def kernel(x, l0_w_in, l0_conv_w, l0_w_br_conv, l0_w_br_attn, l0_pool_mix, l0_pool_scale, l0_w_br_pool, l0_w_o, l0_ln1_g, l0_ln1_b, l0_ffn_gate, l0_ffn_up, l0_ffn_down, l0_ln2_g, l0_ln2_b, l1_w_in, l1_conv_w, l1_w_br_conv, l1_w_br_attn, l1_pool_mix, l1_pool_scale, l1_w_br_pool, l1_w_o, l1_ln1_g, l1_ln1_b, l1_router, l1_exp_gate, l1_exp_up, l1_exp_down, l1_ln2_g, l1_ln2_b):
    batch, seq, d = x.shape
    assert seq == SEQ and d == D_MODEL
    rope_cos, rope_sin = _rope_tables(seq)
    x0 = x.reshape(batch * seq, d)
    x1, x1_bf = _mixer(x0, x0.astype(jnp.bfloat16), batch, l0_w_in, l0_conv_w, l0_w_br_conv, l0_w_br_attn,
                       l0_pool_mix, l0_pool_scale, l0_w_br_pool, l0_w_o, l0_ln1_g, l0_ln1_b, rope_cos, rope_sin)
    x2, x2_bf = _dense_ffn(x1, x1_bf, l0_ffn_gate, l0_ffn_up, l0_ffn_down, l0_ln2_g, l0_ln2_b)
    x3, _ = _mixer(x2, x2_bf, batch, l1_w_in, l1_conv_w, l1_w_br_conv, l1_w_br_attn,
                   l1_pool_mix, l1_pool_scale, l1_w_br_pool, l1_w_o, l1_ln1_g, l1_ln1_b, rope_cos, rope_sin)
    x4 = _moe_ffn(x3, l1_router, l1_exp_gate, l1_exp_up, l1_exp_down, l1_ln2_g, l1_ln2_b)
    return x4.reshape(batch, seq, d)
```

```python
import functools

import jax
import jax.numpy as jnp
from jax import lax
from jax.experimental import pallas as pl
from jax.experimental.pallas import tpu as pltpu

D_MODEL = 2048
SEQ = 4096
CONV_WIDTH = D_MODEL // 2
CONV_KERNEL = 3
HEAD_DIM = 128
HEADS_PER_GROUP = 4
GROUP_WIDTH = HEADS_PER_GROUP * HEAD_DIM
ATTN_PATTERNS = ((128, 1), (512, 4), (2048, 16))
N_GROUPS = len(ATTN_PATTERNS)
ATTN_HEADS = N_GROUPS * HEADS_PER_GROUP
QKV_WIDTH = 3 * N_GROUPS * GROUP_WIDTH
ATTN_BLOCK = 128
ROPE_THETA = 500000.0
ROPE_DIM = HEAD_DIM // 4
POOL_WINDOWS = (2, 4, 8, 16)
POOL_WIDTH = D_MODEL // 2
POOL_GROUP_DIM = POOL_WIDTH // len(POOL_WINDOWS)
POOL_HALO = 16
CONV_HALO = 16
D_FF = 7 * D_MODEL // 2
N_EXPERTS = 8
TOP_K = 2
LN_EPS = 1e-5
DEPTH = 2
DEEPNORM_ALPHA = (2.0 * DEPTH) ** 0.25
NEG_INF = -1e30
LANES = 128

VMEM_LIMIT = 56 * 1024 * 1024


def _params(sem, vmem=VMEM_LIMIT):
    return pltpu.CompilerParams(dimension_semantics=sem, vmem_limit_bytes=vmem)


W_SLAB = 512


def _rope(t, cos, sin):
    half = ROPE_DIM // 2
    lane = lax.broadcasted_iota(jnp.int32, t.shape, 1)
    partner = jnp.where(lane < half, pltpu.roll(t, HEAD_DIM - half, 1), pltpu.roll(t, half, 1))
    return t * cos + partner * sin


def _proj_kernel(*refs, n_slabs, rope_tiles):
    x_ref, w_refs = refs[0], refs[1:1 + n_slabs]
    o_ref, wbf_ref = refs[-2], refs[-1]

    @pl.when(pl.program_id(1) == 0)
    def _():
        for t, w_ref in enumerate(w_refs):
            wbf_ref[:, t * W_SLAB:(t + 1) * W_SLAB] = w_ref[...].astype(jnp.bfloat16)

    def plain():
        o_ref[...] = jnp.dot(x_ref[...], wbf_ref[...],
                             preferred_element_type=jnp.float32).astype(o_ref.dtype)

    if rope_tiles == 0:
        plain()
        return
    cos_ref, sin_ref = refs[1 + n_slabs], refs[2 + n_slabs]
    is_rope_tile = pl.program_id(0) < rope_tiles

    @pl.when(is_rope_tile)
    def _():
        cos, sin = cos_ref[...], sin_ref[...]
        for t in range(n_slabs):
            acc = jnp.dot(x_ref[...], wbf_ref[:, t * W_SLAB:(t + 1) * W_SLAB],
                          preferred_element_type=jnp.float32)
            for h in range(W_SLAB // HEAD_DIM):
                cols = slice(h * HEAD_DIM, (h + 1) * HEAD_DIM)
                out_cols = slice(t * W_SLAB + h * HEAD_DIM, t * W_SLAB + (h + 1) * HEAD_DIM)
                o_ref[:, out_cols] = _rope(acc[:, cols], cos, sin).astype(o_ref.dtype)

    pl.when(jnp.logical_not(is_rope_tile))(plain)


def _project(x_bf, w, col0, width, tn, out_dtype, tm=1024, rope=None, rope_tiles=0):
    m, k = x_bf.shape
    assert col0 % W_SLAB == 0 and tn % W_SLAB == 0 and width % tn == 0 and m % tm == 0
    n_slabs = tn // W_SLAB
    slab0 = col0 // W_SLAB
    w_specs = [pl.BlockSpec((k, W_SLAB), lambda j, i, t=t: (0, slab0 + j * n_slabs + t))
               for t in range(n_slabs)]
    rope_specs, rope_args = [], []
    if rope_tiles:
        tiles_per_seq = SEQ // tm
        rope_specs = [pl.BlockSpec((tm, LANES), lambda j, i: (i % tiles_per_seq, 0))] * 2
        rope_args = list(rope)
    return pl.pallas_call(
        functools.partial(_proj_kernel, n_slabs=n_slabs, rope_tiles=rope_tiles),
        grid=(width // tn, m // tm),
        in_specs=[pl.BlockSpec((tm, k), lambda j, i: (i, 0))] + w_specs + rope_specs,
        out_specs=pl.BlockSpec((tm, tn), lambda j, i: (i, j)),
        out_shape=jax.ShapeDtypeStruct((m, width), out_dtype),
        scratch_shapes=[pltpu.VMEM((k, tn), jnp.bfloat16)],
        compiler_params=_params(("arbitrary", "arbitrary")),
        name="in_proj",
    )(x_bf, *([w] * n_slabs), *rope_args)


def _strided(start, size, stride):
    return pl.ds(start, size) if stride == 1 else pl.ds(start, size, stride=stride)


def _attn_kernel(q0_ref, k0_ref, v0_ref, q1_ref, k1_ref, v1_ref, q2_ref, k2_ref, v2_ref,
                 o_ref, q_s, k_s, v_s, out_s, lse_s, *, chunk, chains):
    groups = ((q0_ref, k0_ref, v0_ref), (q1_ref, k1_ref, v1_ref), (q2_ref, k2_ref, v2_ref))
    n_keys = 2 * ATTN_BLOCK
    a = lax.broadcasted_iota(jnp.int32, (ATTN_BLOCK, n_keys), 0)
    kk = lax.broadcasted_iota(jnp.int32, (ATTN_BLOCK, n_keys), 1)
    scale = HEAD_DIM ** -0.5

    for g, (q_ref, k_ref, v_ref) in enumerate(groups):
        d = ATTN_PATTERNS[g][1]
        if d > 1:
            def stage(ci, carry, q_ref=q_ref, k_ref=k_ref, v_ref=v_ref):
                rows = pl.ds(pl.multiple_of(ci * chunk, chunk), chunk)
                q_s[rows, :] = q_ref[0, rows, :].astype(jnp.float32)
                k_s[rows, :] = k_ref[0, rows, :].astype(jnp.float32)
                v_s[rows, :] = v_ref[0, rows, :].astype(jnp.float32)
                return carry

            lax.fori_loop(0, SEQ // chunk, stage, 0)
            q_src, k_src, v_src = q_s, k_s, v_s
        else:
            q_src, k_src, v_src = q_ref.at[0], k_ref.at[0], v_ref.at[0]

        blocks_per_residue = SEQ // d // ATTN_BLOCK
        unroll = min(chains, blocks_per_residue)
        residues_per_body = chains // unroll
        for c0 in range(0, d, residues_per_body):
            def block(blk, carry, c0=c0, d=d, g=g, residues_per_body=residues_per_body,
                      q_src=q_src, k_src=k_src, v_src=v_src):
                base = pl.multiple_of(blk * (ATTN_BLOCK * d), ATTN_BLOCK)
                key_base = pl.multiple_of(jnp.maximum(base - ATTN_BLOCK * d, 0), ATTN_BLOCK)
                self_key = jnp.where(blk == 0, 0, ATTN_BLOCK)
                valid = (kk >= a + self_key - ATTN_BLOCK) & (kk <= a + self_key)
                for c in range(c0, c0 + residues_per_body):
                    qb = q_src[_strided(base + c, ATTN_BLOCK, d), :].astype(jnp.bfloat16)
                    kb = k_src[_strided(key_base + c, n_keys, d), :].astype(jnp.bfloat16)
                    vb = v_src[_strided(key_base + c, n_keys, d), :].astype(jnp.bfloat16)
                    s = lax.dot_general(qb, kb, (((1,), (1,)), ((), ())),
                                        preferred_element_type=jnp.float32) * scale
                    s = jnp.where(valid, s, NEG_INF)
                    m = jnp.max(s, axis=-1, keepdims=True)
                    p = jnp.exp(s - m)
                    l = jnp.sum(p, axis=-1, keepdims=True)
                    o = jnp.dot(p.astype(jnp.bfloat16), vb, preferred_element_type=jnp.float32)
                    rows = _strided(base + c, ATTN_BLOCK, d)
                    out_s[g, rows, :] = o / l
                    lse_s[g, rows, :] = jnp.broadcast_to(m + jnp.log(l), (ATTN_BLOCK, HEAD_DIM))
                return carry

            lax.fori_loop(0, blocks_per_residue, block, 0, unroll=unroll)

    def combine(ci, carry):
        rows = pl.ds(pl.multiple_of(ci * chunk, chunk), chunk)
        l0, l1, l2 = lse_s[0, rows, :], lse_s[1, rows, :], lse_s[2, rows, :]
        mx = jnp.maximum(jnp.maximum(l0, l1), l2)
        e0, e1, e2 = jnp.exp(l0 - mx), jnp.exp(l1 - mx), jnp.exp(l2 - mx)
        num = e0 * out_s[0, rows, :] + e1 * out_s[1, rows, :] + e2 * out_s[2, rows, :]
        o_ref[0, rows, :] = (num / (e0 + e1 + e2)).astype(o_ref.dtype)
        return carry

    lax.fori_loop(0, SEQ // chunk, combine, 0)


def _attention(qkv, chunk=256, chains=8):
    b, s, _ = qkv.shape
    assert s == SEQ

    def head_block(part, group):
        col0 = part * ATTN_HEADS + group * HEADS_PER_GROUP
        return pl.BlockSpec((1, s, HEAD_DIM), lambda bi, h: (bi, 0, col0 + h))

    in_specs = [head_block(part, group) for group in range(N_GROUPS) for part in range(3)]
    return pl.pallas_call(
        functools.partial(_attn_kernel, chunk=chunk, chains=chains),
        grid=(b, HEADS_PER_GROUP),
        in_specs=in_specs,
        out_specs=pl.BlockSpec((1, s, HEAD_DIM), lambda bi, h: (bi, 0, h)),
        out_shape=jax.ShapeDtypeStruct((b, s, GROUP_WIDTH), jnp.bfloat16),
        scratch_shapes=[pltpu.VMEM((s, HEAD_DIM), jnp.float32),
                        pltpu.VMEM((s, HEAD_DIM), jnp.float32),
                        pltpu.VMEM((s, HEAD_DIM), jnp.float32),
                        pltpu.VMEM((N_GROUPS, s, HEAD_DIM), jnp.float32),
                        pltpu.VMEM((N_GROUPS, s, HEAD_DIM), jnp.float32)],
        compiler_params=_params(("arbitrary", "arbitrary")),
        name="dilated_attn",
    )(*([qkv] * 9))


def _resid_layer_norm(x, f, g, b):
    z = DEEPNORM_ALPHA * x + f
    mu = jnp.mean(z, axis=-1, keepdims=True)
    zc = z - mu
    var = jnp.mean(zc * zc, axis=-1, keepdims=True)
    return zc * lax.rsqrt(var + LN_EPS) * g + b


def _mixer_tail_kernel(cb_ref, cc_ref, ch_ref, cch_ref, chh_ref, pu_ref, puh_ref, attn_ref,
                       g0a_ref, g0b_ref, g1a_ref, g1b_ref, g2a_ref, g2b_ref, x_ref,
                       convw_ref, pscale_ref, lng_ref, lnb_ref,
                       wconv_ref, wattn_ref, wpool_ref, pmix_ref, wo_ref, o_ref, obf_ref,
                       *, tm, tiles_per_seq):
    seq_tile = pl.program_id(0) % tiles_per_seq
    not_first = (seq_tile != 0).astype(jnp.float32)
    f32 = lambda ref: ref[...].astype(jnp.float32)
    gate = lambda lo_ref, hi_ref: jax.nn.sigmoid(jnp.concatenate([f32(lo_ref), f32(hi_ref)], axis=1))

    u = f32(cc_ref) * f32(ch_ref)
    uh = f32(cch_ref) * f32(chh_ref) * not_first
    ext = jnp.concatenate([uh, u], axis=0)
    u1 = pltpu.roll(ext, 1, 0)[CONV_HALO:]
    u2 = pltpu.roll(ext, 2, 0)[CONV_HALO:]
    cw = convw_ref[...]
    conv = u2 * cw[0:1] + u1 * cw[1:2]
    conv = conv + u * cw[2:3]
    ya_in = (f32(cb_ref) * conv).astype(jnp.bfloat16)
    y_a = jnp.dot(ya_in, wconv_ref[...], preferred_element_type=jnp.float32)
    merged = gate(g0a_ref, g0b_ref) * y_a

    y_b = jnp.dot(attn_ref[...], wattn_ref[...], preferred_element_type=jnp.float32)
    merged = merged + gate(g1a_ref, g1b_ref) * y_b

    pu = f32(pu_ref)
    pext = jnp.concatenate([f32(puh_ref) * not_first, pu], axis=0)
    pos = seq_tile * tm + lax.broadcasted_iota(jnp.int32, (tm, 1), 0) + 1
    pscale = pscale_ref[...]
    y_c = None
    for gi, w in enumerate(POOL_WINDOWS):
        cols = slice(gi * POOL_GROUP_DIM, (gi + 1) * POOL_GROUP_DIM)
        run = pext[:, cols]
        step = 1
        while step < w:
            run = run + pltpu.roll(run, step, 0)
            step *= 2
        inv_cnt = 1.0 / jnp.minimum(pos, w).astype(jnp.float32)
        pooled = run[POOL_HALO:] * inv_cnt - pu[:, cols]
        mixed = jnp.dot(pooled.astype(jnp.bfloat16), pmix_ref[gi], preferred_element_type=jnp.float32)
        mixed = (mixed * pscale[:, cols]).astype(jnp.bfloat16)
        part = jnp.dot(mixed, wpool_ref[cols, :], preferred_element_type=jnp.float32)
        y_c = part if y_c is None else y_c + part
    merged = merged + gate(g2a_ref, g2b_ref) * y_c

    f = jnp.dot(merged.astype(jnp.bfloat16), wo_ref[...], preferred_element_type=jnp.float32)
    y = _resid_layer_norm(x_ref[...], f, lng_ref[...], lnb_ref[...])
    o_ref[...] = y
    obf_ref[...] = y.astype(jnp.bfloat16)


def _mixer_tail(conv_p, pool_gates_p, attn, x, conv_w, pool_scale, ln_g, ln_b,
                w_conv_bf, w_attn_bf, w_pool_bf, pool_mix_bf, w_o_bf, tm=256):
    m = conv_p.shape[0]
    gate_blocks = N_GROUPS * D_MODEL // CONV_WIDTH
    tiles_per_seq = SEQ // tm
    row = lambda width, col: pl.BlockSpec((tm, width), lambda i: (i, col))

    def halo(rows, col):
        per_tile = tm // rows
        return pl.BlockSpec((rows, CONV_WIDTH), lambda i: (jnp.maximum(i * per_tile - 1, 0), col))

    full = lambda shape: pl.BlockSpec(shape, lambda i: (0,) * len(shape), pipeline_mode=pl.Buffered(1))
    in_specs = [row(CONV_WIDTH, 0), row(CONV_WIDTH, 1), row(CONV_WIDTH, 2),
                halo(CONV_HALO, 1), halo(CONV_HALO, 2),
                row(POOL_WIDTH, 0), halo(POOL_HALO, 0),
                row(GROUP_WIDTH, 0)]
    in_specs += [row(CONV_WIDTH, 1 + gb) for gb in range(gate_blocks)]
    in_specs += [row(D_MODEL, 0),
                 full((CONV_KERNEL, CONV_WIDTH)), full((1, POOL_WIDTH)), full((1, D_MODEL)), full((1, D_MODEL)),
                 full((CONV_WIDTH, D_MODEL)), full((GROUP_WIDTH, D_MODEL)), full((POOL_WIDTH, D_MODEL)),
                 full((len(POOL_WINDOWS), POOL_GROUP_DIM, POOL_GROUP_DIM)), full((D_MODEL, D_MODEL))]
    return pl.pallas_call(
        functools.partial(_mixer_tail_kernel, tm=tm, tiles_per_seq=tiles_per_seq),
        grid=(m // tm,),
        in_specs=in_specs,
        out_specs=[row(D_MODEL, 0), row(D_MODEL, 0)],
        out_shape=[jax.ShapeDtypeStruct((m, D_MODEL), jnp.float32),
                   jax.ShapeDtypeStruct((m, D_MODEL), jnp.bfloat16)],
        compiler_params=_params(("arbitrary",)),
        name="mixer_tail",
    )(conv_p, conv_p, conv_p, conv_p, conv_p, pool_gates_p, pool_gates_p, attn,
      *([pool_gates_p] * gate_blocks), x,
      conv_w, pool_scale.reshape(1, POOL_WIDTH), ln_g.reshape(1, D_MODEL), ln_b.reshape(1, D_MODEL),
      w_conv_bf, w_attn_bf, w_pool_bf, pool_mix_bf, w_o_bf)


def _route_tile(y, w_ref, idx_ref, gate_ref, rank_ref, cnt_ref, carry_ref, *, tm):
    @pl.when(pl.program_id(0) == 0)
    def _():
        carry_ref[...] = jnp.zeros_like(carry_ref)

    w = w_ref[...]
    y_hi, w_hi = y.astype(jnp.bfloat16), w.astype(jnp.bfloat16)
    y_lo = (y - y_hi.astype(jnp.float32)).astype(jnp.bfloat16)
    w_lo = (w - w_hi.astype(jnp.float32)).astype(jnp.bfloat16)
    logits = (jnp.dot(y_hi, w_hi, preferred_element_type=jnp.float32)
              + jnp.dot(y_lo, w_hi, preferred_element_type=jnp.float32)
              + jnp.dot(y_hi, w_lo, preferred_element_type=jnp.float32))
    lane = lax.broadcasted_iota(jnp.int32, logits.shape, 1)
    logits = jnp.where(lane < N_EXPERTS, logits, -jnp.inf)
    m1 = jnp.max(logits, axis=-1, keepdims=True)
    i1 = jnp.min(jnp.where(logits == m1, lane, LANES), axis=-1, keepdims=True)
    rest = jnp.where(lane == i1, -jnp.inf, logits)
    m2 = jnp.max(rest, axis=-1, keepdims=True)
    i2 = jnp.min(jnp.where(rest == m2, lane, LANES), axis=-1, keepdims=True)
    e = jnp.exp(m2 - m1)
    den = 1.0 + e
    idx_ref[:, 0:1] = i1
    idx_ref[:, 1:2] = i2
    gate_ref[:, 0:1] = 1.0 / den
    gate_ref[:, 1:2] = e / den

    oh1 = (lane == i1).astype(jnp.float32)
    oh2 = (lane == i2).astype(jnp.float32)
    both = oh1 + oh2
    r = lax.broadcasted_iota(jnp.int32, (tm, tm), 0)
    c = lax.broadcasted_iota(jnp.int32, (tm, tm), 1)
    strict_lower = (r > c).astype(jnp.bfloat16)
    before = jnp.dot(strict_lower, both.astype(jnp.bfloat16), preferred_element_type=jnp.float32)
    before = before + carry_ref[...]
    rank_ref[:, 0:1] = jnp.sum(oh1 * before, axis=-1, keepdims=True).astype(jnp.int32)
    rank_ref[:, 1:2] = jnp.sum(oh2 * before, axis=-1, keepdims=True).astype(jnp.int32)
    carry_ref[...] += jnp.sum(both, axis=0, keepdims=True)
    cnt_ref[...] = carry_ref[...]


def _route_kernel(x_ref, w_ref, idx_ref, gate_ref, rank_ref, cnt_ref, carry_ref, *, tm):
    _route_tile(x_ref[...], w_ref, idx_ref, gate_ref, rank_ref, cnt_ref, carry_ref, tm=tm)


def _route(x, w_router, tm=512):
    m = x.shape[0]
    w_pad = jnp.pad(w_router, ((0, 0), (0, LANES - N_EXPERTS)))
    pair = pl.BlockSpec((tm, TOP_K), lambda i: (i, 0))
    return pl.pallas_call(
        functools.partial(_route_kernel, tm=tm),
        grid=(m // tm,),
        in_specs=[pl.BlockSpec((tm, D_MODEL), lambda i: (i, 0)),
                  pl.BlockSpec((D_MODEL, LANES), lambda i: (0, 0))],
        out_specs=[pair, pair, pair, pl.BlockSpec((1, LANES), lambda i: (0, 0))],
        out_shape=[jax.ShapeDtypeStruct((m, TOP_K), jnp.int32),
                   jax.ShapeDtypeStruct((m, TOP_K), jnp.float32),
                   jax.ShapeDtypeStruct((m, TOP_K), jnp.int32),
                   jax.ShapeDtypeStruct((1, LANES), jnp.float32)],
        scratch_shapes=[pltpu.VMEM((1, LANES), jnp.float32)],
        compiler_params=_params(("arbitrary",)),
        name="router",
    )(x, w_pad)


def _swiglu_kernel(blk_exp_ref, blk_rows_ref, n_used_ref, x_ref, wg_ref, wu_ref, wd_ref, *refs,
                   tm, sub, last_j, resid_ln):
    del blk_exp_ref, n_used_ref
    if resid_ln:
        res_ref, g_ref, b_ref, o_ref, obf_ref = refs[:5]
        scratch = refs[5:]
    else:
        o_ref, scratch = refs[0], refs[1:]
    xbf_ref = scratch[0] if scratch else x_ref
    i, j = pl.program_id(0), pl.program_id(1)
    n_sub = (blk_rows_ref[i] + sub - 1) // sub

    @pl.when(j == 0)
    def _():
        if scratch:
            xbf_ref[...] = x_ref[...].astype(jnp.bfloat16)
        o_ref[...] = jnp.zeros_like(o_ref)

    for v in range(1, tm // sub + 1):
        rows = v * sub

        @pl.when(n_sub == v)
        def _(rows=rows):
            x = xbf_ref[0:rows, :]
            gate = jnp.dot(x, wg_ref[0].astype(jnp.bfloat16), preferred_element_type=jnp.float32)
            up = jnp.dot(x, wu_ref[0].astype(jnp.bfloat16), preferred_element_type=jnp.float32)
            h = (gate * jax.nn.sigmoid(gate) * up).astype(jnp.bfloat16)
            o_ref[0:rows, :] += jnp.dot(h, wd_ref[0].astype(jnp.bfloat16),
                                        preferred_element_type=jnp.float32)

    if resid_ln:
        @pl.when(j == last_j)
        def _():
            def norm_rows(ci, carry):
                rows = pl.ds(pl.multiple_of(ci * sub, sub), sub)
                y = _resid_layer_norm(res_ref[rows, :], o_ref[rows, :], g_ref[...], b_ref[...])
                o_ref[rows, :] = y
                obf_ref[rows, :] = y.astype(jnp.bfloat16)
                return carry

            lax.fori_loop(0, tm // sub, norm_rows, 0)


def _swiglu_blocks(x, w_gate, w_up, w_down, blk_exp, blk_rows, n_used, tm, tf, sub=256, resid_ln=None):
    r = x.shape[0]
    nf = D_FF // tf
    last = nf - 1

    def live(i, nu):
        return jnp.maximum(jnp.minimum(i, nu[0] - 1), 0)

    def jj(i, j, nu):
        return jnp.where(i < nu[0], j, last)

    once = pl.Buffered(1)
    row_in = pl.BlockSpec((tm, D_MODEL), lambda i, j, be, br, nu: (live(i, nu), 0), pipeline_mode=once)
    row_out = pl.BlockSpec((tm, D_MODEL), lambda i, j, be, br, nu: (i, 0), pipeline_mode=once)
    vec = pl.BlockSpec((1, D_MODEL), lambda i, j, be, br, nu: (0, 0))
    in_specs = [row_in,
                pl.BlockSpec((1, D_MODEL, tf), lambda i, j, be, br, nu: (be[live(i, nu)], 0, jj(i, j, nu))),
                pl.BlockSpec((1, D_MODEL, tf), lambda i, j, be, br, nu: (be[live(i, nu)], 0, jj(i, j, nu))),
                pl.BlockSpec((1, tf, D_MODEL), lambda i, j, be, br, nu: (be[live(i, nu)], jj(i, j, nu), 0))]
    args = [x, w_gate, w_up, w_down]
    out_specs = row_out
    out_shape = jax.ShapeDtypeStruct((r, D_MODEL), jnp.float32)
    if resid_ln is not None:
        res, g, b = resid_ln
        in_specs += [row_in, vec, vec]
        args += [res, g.reshape(1, D_MODEL), b.reshape(1, D_MODEL)]
        out_specs = [row_out, row_out]
        out_shape = [out_shape, jax.ShapeDtypeStruct((r, D_MODEL), jnp.bfloat16)]
    scratch = [] if x.dtype == jnp.bfloat16 else [pltpu.VMEM((tm, D_MODEL), jnp.bfloat16)]
    grid_spec = pltpu.PrefetchScalarGridSpec(
        num_scalar_prefetch=3, grid=(r // tm, nf),
        in_specs=in_specs, out_specs=out_specs, scratch_shapes=scratch)
    return pl.pallas_call(
        functools.partial(_swiglu_kernel, tm=tm, sub=sub, last_j=last, resid_ln=resid_ln is not None),
        grid_spec=grid_spec,
        out_shape=out_shape,
        compiler_params=_params(("arbitrary", "arbitrary")),
        name="swiglu_blocks",
    )(blk_exp, blk_rows, n_used, *args)


ZERO_ROWS = 256


def _dispatch_kernel(zero_slab_ref, dest_ref, x_ref, xs_ref, zero_ref, sem, *, tm):
    @pl.when(pl.program_id(0) == 0)
    def _():
        zero_ref[...] = jnp.zeros_like(zero_ref)
        slabs = [pltpu.make_async_copy(zero_ref, xs_ref.at[pl.ds(z * ZERO_ROWS, ZERO_ROWS)], sem)
                 for z in range(zero_slab_ref.shape[0])]
        for z, slab in enumerate(slabs):
            pl.when(zero_slab_ref[z] != 0)(slab.start)
        for z, slab in enumerate(slabs):
            pl.when(zero_slab_ref[z] != 0)(slab.wait)

    def issue(t, carry):
        for k in range(TOP_K):
            pltpu.make_async_copy(x_ref.at[pl.ds(t, 1)],
                                  xs_ref.at[pl.ds(dest_ref[TOP_K * t + k], 1)], sem).start()
        return carry

    lax.fori_loop(0, tm, issue, 0, unroll=8)
    for k in range(TOP_K):
        pltpu.make_async_copy(x_ref, xs_ref.at[pl.ds(0, tm)], sem).wait()


def _dispatch(x, dest_flat, zero_slab, rows, tm=256):
    m = x.shape[0]
    grid_spec = pltpu.PrefetchScalarGridSpec(
        num_scalar_prefetch=1, grid=(m // tm,),
        in_specs=[pl.BlockSpec((TOP_K * tm,), lambda i, zs: (i,), memory_space=pltpu.SMEM),
                  pl.BlockSpec((tm, D_MODEL), lambda i, zs: (i, 0))],
        out_specs=pl.BlockSpec(memory_space=pl.ANY),
        scratch_shapes=[pltpu.VMEM((ZERO_ROWS, D_MODEL), x.dtype), pltpu.SemaphoreType.DMA(())])
    return pl.pallas_call(
        functools.partial(_dispatch_kernel, tm=tm),
        grid_spec=grid_spec,
        out_shape=jax.ShapeDtypeStruct((rows, D_MODEL), x.dtype),
        compiler_params=_params(("arbitrary",)),
        name="dispatch",
    )(zero_slab, dest_flat, x)


def _combine_ln_kernel(dest_ref, dest_next_ref, x_ref, gate_ref, g_ref, b_ref, y_ref, o_ref,
                       ybuf_ref, sems, *, tm):
    i, n = pl.program_id(0), pl.num_programs(0)
    slot = lax.rem(i, 2)

    def gather(d_ref, s):
        def issue(t, carry):
            for k in range(TOP_K):
                pltpu.make_async_copy(y_ref.at[pl.ds(d_ref[TOP_K * t + k], 1)],
                                      ybuf_ref.at[s, k, pl.ds(t, 1)], sems.at[s]).start()
            return carry

        lax.fori_loop(0, tm, issue, 0, unroll=8)

    @pl.when(i == 0)
    def _():
        gather(dest_ref, 0)

    @pl.when(i + 1 < n)
    def _():
        gather(dest_next_ref, 1 - slot)

    for k in range(TOP_K):
        pltpu.make_async_copy(y_ref.at[pl.ds(0, tm)], ybuf_ref.at[slot, k], sems.at[slot]).wait()
    gate = gate_ref[...]
    f = ybuf_ref[slot, 0] * gate[:, 0:1] + ybuf_ref[slot, 1] * gate[:, 1:2]
    o_ref[...] = _resid_layer_norm(x_ref[...], f, g_ref[...], b_ref[...])


def _combine_ln(x, y, dest_flat, gates, g, b, tm=256):
    m = x.shape[0]
    n_tiles = m // tm
    row = pl.BlockSpec((tm, D_MODEL), lambda i: (i, 0))
    vec = pl.BlockSpec((1, D_MODEL), lambda i: (0, 0))
    return pl.pallas_call(
        functools.partial(_combine_ln_kernel, tm=tm),
        grid=(n_tiles,),
        in_specs=[pl.BlockSpec((TOP_K * tm,), lambda i: (i,), memory_space=pltpu.SMEM),
                  pl.BlockSpec((TOP_K * tm,), lambda i: (jnp.minimum(i + 1, n_tiles - 1),),
                               memory_space=pltpu.SMEM),
                  row, pl.BlockSpec((tm, TOP_K), lambda i: (i, 0)), vec, vec,
                  pl.BlockSpec(memory_space=pl.ANY)],
        out_specs=row,
        out_shape=jax.ShapeDtypeStruct((m, D_MODEL), jnp.float32),
        scratch_shapes=[pltpu.VMEM((2, TOP_K, tm, D_MODEL), jnp.float32),
                        pltpu.SemaphoreType.DMA((2,))],
        compiler_params=_params(("arbitrary",)),
        name="combine_ln",
    )(dest_flat, dest_flat, x, gates, g.reshape(1, D_MODEL), b.reshape(1, D_MODEL), y)


def _rope_tables(seq):
    inv_freq = ROPE_THETA ** (-jnp.arange(0, ROPE_DIM, 2, dtype=jnp.float32) / ROPE_DIM)
    ang = jnp.arange(seq, dtype=jnp.float32)[:, None] * inv_freq[None, :]
    cos, sin = jnp.cos(ang), jnp.sin(ang)
    pad = HEAD_DIM - ROPE_DIM
    cos_full = jnp.concatenate([cos, cos, jnp.ones((seq, pad), jnp.float32)], axis=1)
    sin_full = jnp.concatenate([-sin, sin, jnp.zeros((seq, pad), jnp.float32)], axis=1)
    return cos_full, sin_full


def _mixer(x, x_bf, batch, w_in, conv_w, w_br_conv, w_br_attn, pool_mix, pool_scale, w_br_pool, w_o,
           ln_g, ln_b, rope_cos, rope_sin):
    conv_cols = CONV_KERNEL * CONV_WIDTH
    pool_col = conv_cols + QKV_WIDTH
    conv_p = _project(x_bf, w_in, 0, conv_cols, 1024, jnp.bfloat16)
    qkv = _project(x_bf, w_in, conv_cols, QKV_WIDTH, QKV_WIDTH // 3, jnp.bfloat16,
                   rope=(rope_cos, rope_sin), rope_tiles=2)
    pool_gates_p = _project(x_bf, w_in, pool_col, POOL_WIDTH + N_GROUPS * D_MODEL, 1024, jnp.bfloat16)
    attn = _attention(qkv.reshape(batch, SEQ, QKV_WIDTH))
    attn = attn.reshape(batch * SEQ, GROUP_WIDTH)
    bf = lambda w: w.astype(jnp.bfloat16)
    return _mixer_tail(conv_p, pool_gates_p, attn, x, conv_w, pool_scale, ln_g, ln_b,
                       bf(w_br_conv), bf(w_br_attn), bf(w_br_pool), bf(pool_mix), bf(w_o))


def _dense_ffn(x, x_bf, w_gate, w_up, w_down, ln_g, ln_b, tm=1024, tf=512):
    n_blk = x.shape[0] // tm
    blk_exp = jnp.zeros((n_blk,), jnp.int32)
    blk_rows = jnp.full((n_blk,), tm, jnp.int32)
    n_used = jnp.full((1,), n_blk, jnp.int32)
    return _swiglu_blocks(x_bf, w_gate[None], w_up[None], w_down[None], blk_exp, blk_rows, n_used, tm, tf,
                          resid_ln=(x, ln_g, ln_b))


def _moe_ffn(x, w_router, w_gate, w_up, w_down, ln_g, ln_b, tm=1024, tf=512):
    m = x.shape[0]
    idx, gates, rank, cnt = _route(x, w_router)
    counts = cnt[0, :N_EXPERTS].astype(jnp.int32)
    padded = (counts + tm - 1) // tm * tm
    pend = jnp.cumsum(padded)
    pstart = pend - padded
    dest_flat = (pstart[idx] + rank).reshape(-1)
    n_blk = m * TOP_K // tm + N_EXPERTS
    blk_start = jnp.arange(n_blk, dtype=jnp.int32) * tm
    blk_exp = jnp.minimum(jnp.sum(blk_start[:, None] >= pend[None, :], axis=1), N_EXPERTS - 1).astype(jnp.int32)
    blk_rows = jnp.clip(pstart[blk_exp] + counts[blk_exp] - blk_start, 0, tm).astype(jnp.int32)
    n_used = (pend[-1:] // tm).astype(jnp.int32)
    blk_rows = jnp.where(jnp.arange(n_blk) < n_used[0], blk_rows, 0)
    slab_end = (jnp.arange(n_blk * tm // ZERO_ROWS, dtype=jnp.int32) + 1) * ZERO_ROWS
    slab_exp = jnp.repeat(blk_exp, tm // ZERO_ROWS)
    zero_slab = (slab_end > pstart[slab_exp] + counts[slab_exp]).astype(jnp.int32)
    xs = _dispatch(x, dest_flat, zero_slab, n_blk * tm)
    ys = _swiglu_blocks(xs, w_gate, w_up, w_down, blk_exp, blk_rows, n_used, tm, tf)
    return _combine_ln(x, ys, dest_flat, gates, ln_g, ln_b)


def kernel(x, l0_w_in, l0_conv_w, l0_w_br_conv, l0_w_br_attn, l0_pool_mix, l0_pool_scale, l0_w_br_pool, l0_w_o, l0_ln1_g, l0_ln1_b, l0_ffn_gate, l0_ffn_up, l0_ffn_down, l0_ln2_g, l0_ln2_b, l1_w_in, l1_conv_w, l1_w_br_conv, l1_w_br_attn, l1_pool_mix, l1_pool_scale, l1_w_br_pool, l1_w_o, l1_ln1_g, l1_ln1_b, l1_router, l1_exp_gate, l1_exp_up, l1_exp_down, l1_ln2_g, l1_ln2_b):
    batch, seq, d = x.shape
    assert seq == SEQ and d == D_MODEL
    rope_cos, rope_sin = _rope_tables(seq)
    x0 = x.reshape(batch * seq, d)
    x1, x1_bf = _mixer(x0, x0.astype(jnp.bfloat16), batch, l0_w_in, l0_conv_w, l0_w_br_conv, l0_w_br_attn,
                       l0_pool_mix, l0_pool_scale, l0_w_br_pool, l0_w_o, l0_ln1_g, l0_ln1_b, rope_cos, rope_sin)
    x2, x2_bf = _dense_ffn(x1, x1_bf, l0_ffn_gate, l0_ffn_up, l0_ffn_down, l0_ln2_g, l0_ln2_b)
    x3, _ = _mixer(x2, x2_bf, batch, l1_w_in, l1_conv_w, l1_w_br_conv, l1_w_br_attn,
                   l1_pool_mix, l1_pool_scale, l1_w_br_pool, l1_w_o, l1_ln1_g, l1_ln1_b, rope_cos, rope_sin)
    x4 = _moe_ffn(x3, l1_router, l1_exp_gate, l1_exp_up, l1_exp_down, l1_ln2_g, l1_ln2_b)
    return x4.reshape(batch, seq, d)
```

```python
import functools

import jax
import jax.numpy as jnp
from jax import lax
from jax.experimental import pallas as pl
from jax.experimental.pallas import tpu as pltpu

D_MODEL = 2048
SEQ = 4096
CONV_WIDTH = D_MODEL // 2
CONV_KERNEL = 3
HEAD_DIM = 128
HEADS_PER_GROUP = 4
GROUP_WIDTH = HEADS_PER_GROUP * HEAD_DIM
ATTN_PATTERNS = ((128, 1), (512, 4), (2048, 16))
N_GROUPS = len(ATTN_PATTERNS)
ATTN_HEADS = N_GROUPS * HEADS_PER_GROUP
QKV_WIDTH = 3 * N_GROUPS * GROUP_WIDTH
ATTN_BLOCK = 128
ROPE_THETA = 500000.0
ROPE_DIM = HEAD_DIM // 4
POOL_WINDOWS = (2, 4, 8, 16)
POOL_WIDTH = D_MODEL // 2
POOL_GROUP_DIM = POOL_WIDTH // len(POOL_WINDOWS)
POOL_HALO = 16
CONV_HALO = 16
D_FF = 7 * D_MODEL // 2
N_EXPERTS = 8
TOP_K = 2
LN_EPS = 1e-5
DEPTH = 2
DEEPNORM_ALPHA = (2.0 * DEPTH) ** 0.25
NEG_INF = -1e30
LANES = 128

VMEM_LIMIT = 56 * 1024 * 1024


def _params(sem, vmem=VMEM_LIMIT):
    return pltpu.CompilerParams(dimension_semantics=sem, vmem_limit_bytes=vmem)


W_SLAB = 512


def _rope(t, cos, sin):
    half = ROPE_DIM // 2
    lane = lax.broadcasted_iota(jnp.int32, t.shape, 1)
    partner = jnp.where(lane < half, pltpu.roll(t, HEAD_DIM - half, 1), pltpu.roll(t, half, 1))
    return t * cos + partner * sin


def _proj_kernel(*refs, n_slabs, rope_tiles):
    x_ref, w_refs = refs[0], refs[1:1 + n_slabs]
    o_ref, wbf_ref = refs[-2], refs[-1]

    @pl.when(pl.program_id(1) == 0)
    def _():
        for t, w_ref in enumerate(w_refs):
            wbf_ref[:, t * W_SLAB:(t + 1) * W_SLAB] = w_ref[...].astype(jnp.bfloat16)

    def plain():
        o_ref[...] = jnp.dot(x_ref[...], wbf_ref[...],
                             preferred_element_type=jnp.float32).astype(o_ref.dtype)

    if rope_tiles == 0:
        plain()
        return
    cos_ref, sin_ref = refs[1 + n_slabs], refs[2 + n_slabs]
    is_rope_tile = pl.program_id(0) < rope_tiles

    @pl.when(is_rope_tile)
    def _():
        cos, sin = cos_ref[...], sin_ref[...]
        for t in range(n_slabs):
            acc = jnp.dot(x_ref[...], wbf_ref[:, t * W_SLAB:(t + 1) * W_SLAB],
                          preferred_element_type=jnp.float32)
            for h in range(W_SLAB // HEAD_DIM):
                cols = slice(h * HEAD_DIM, (h + 1) * HEAD_DIM)
                out_cols = slice(t * W_SLAB + h * HEAD_DIM, t * W_SLAB + (h + 1) * HEAD_DIM)
                o_ref[:, out_cols] = _rope(acc[:, cols], cos, sin).astype(o_ref.dtype)

    pl.when(jnp.logical_not(is_rope_tile))(plain)


def _project(x_bf, w, col0, width, tn, out_dtype, tm=1024, rope=None, rope_tiles=0):
    m, k = x_bf.shape
    assert col0 % W_SLAB == 0 and tn % W_SLAB == 0 and width % tn == 0 and m % tm == 0
    n_slabs = tn // W_SLAB
    slab0 = col0 // W_SLAB
    w_specs = [pl.BlockSpec((k, W_SLAB), lambda j, i, t=t: (0, slab0 + j * n_slabs + t))
               for t in range(n_slabs)]
    rope_specs, rope_args = [], []
    if rope_tiles:
        tiles_per_seq = SEQ // tm
        rope_specs = [pl.BlockSpec((tm, LANES), lambda j, i: (i % tiles_per_seq, 0))] * 2
        rope_args = list(rope)
    return pl.pallas_call(
        functools.partial(_proj_kernel, n_slabs=n_slabs, rope_tiles=rope_tiles),
        grid=(width // tn, m // tm),
        in_specs=[pl.BlockSpec((tm, k), lambda j, i: (i, 0))] + w_specs + rope_specs,
        out_specs=pl.BlockSpec((tm, tn), lambda j, i: (i, j)),
        out_shape=jax.ShapeDtypeStruct((m, width), out_dtype),
        scratch_shapes=[pltpu.VMEM((k, tn), jnp.bfloat16)],
        compiler_params=_params(("arbitrary", "arbitrary")),
        name="in_proj",
    )(x_bf, *([w] * n_slabs), *rope_args)


def _strided(start, size, stride):
    return pl.ds(start, size) if stride == 1 else pl.ds(start, size, stride=stride)


def _attn_kernel(q0_ref, k0_ref, v0_ref, q1_ref, k1_ref, v1_ref, q2_ref, k2_ref, v2_ref,
                 o_ref, q_s, k_s, v_s, out_s, lse_s, *, chunk, chains):
    groups = ((q0_ref, k0_ref, v0_ref), (q1_ref, k1_ref, v1_ref), (q2_ref, k2_ref, v2_ref))
    n_keys = 2 * ATTN_BLOCK
    a = lax.broadcasted_iota(jnp.int32, (ATTN_BLOCK, n_keys), 0)
    kk = lax.broadcasted_iota(jnp.int32, (ATTN_BLOCK, n_keys), 1)
    scale = HEAD_DIM ** -0.5

    for g, (q_ref, k_ref, v_ref) in enumerate(groups):
        d = ATTN_PATTERNS[g][1]
        if d > 1:
            def stage(ci, carry, q_ref=q_ref, k_ref=k_ref, v_ref=v_ref):
                rows = pl.ds(pl.multiple_of(ci * chunk, chunk), chunk)
                q_s[rows, :] = q_ref[0, rows, :].astype(jnp.float32)
                k_s[rows, :] = k_ref[0, rows, :].astype(jnp.float32)
                v_s[rows, :] = v_ref[0, rows, :].astype(jnp.float32)
                return carry

            lax.fori_loop(0, SEQ // chunk, stage, 0)
            q_src, k_src, v_src = q_s, k_s, v_s
        else:
            q_src, k_src, v_src = q_ref.at[0], k_ref.at[0], v_ref.at[0]

        blocks_per_residue = SEQ // d // ATTN_BLOCK
        unroll = min(chains, blocks_per_residue)
        residues_per_body = chains // unroll
        for c0 in range(0, d, residues_per_body):
            def block(blk, carry, c0=c0, d=d, g=g, residues_per_body=residues_per_body,
                      q_src=q_src, k_src=k_src, v_src=v_src):
                base = pl.multiple_of(blk * (ATTN_BLOCK * d), ATTN_BLOCK)
                key_base = pl.multiple_of(jnp.maximum(base - ATTN_BLOCK * d, 0), ATTN_BLOCK)
                self_key = jnp.where(blk == 0, 0, ATTN_BLOCK)
                valid = (kk >= a + self_key - ATTN_BLOCK) & (kk <= a + self_key)
                for c in range(c0, c0 + residues_per_body):
                    qb = q_src[_strided(base + c, ATTN_BLOCK, d), :].astype(jnp.bfloat16)
                    kb = k_src[_strided(key_base + c, n_keys, d), :].astype(jnp.bfloat16)
                    vb = v_src[_strided(key_base + c, n_keys, d), :].astype(jnp.bfloat16)
                    s = lax.dot_general(qb, kb, (((1,), (1,)), ((), ())),
                                        preferred_element_type=jnp.float32) * scale
                    s = jnp.where(valid, s, NEG_INF)
                    m = jnp.max(s, axis=-1, keepdims=True)
                    p = jnp.exp(s - m)
                    l = jnp.sum(p, axis=-1, keepdims=True)
                    o = jnp.dot(p.astype(jnp.bfloat16), vb, preferred_element_type=jnp.float32)
                    rows = _strided(base + c, ATTN_BLOCK, d)
                    out_s[g, rows, :] = o / l
                    lse_s[g, rows, :] = jnp.broadcast_to(m + jnp.log(l), (ATTN_BLOCK, HEAD_DIM))
                return carry

            lax.fori_loop(0, blocks_per_residue, block, 0, unroll=unroll)

    def combine(ci, carry):
        rows = pl.ds(pl.multiple_of(ci * chunk, chunk), chunk)
        l0, l1, l2 = lse_s[0, rows, :], lse_s[1, rows, :], lse_s[2, rows, :]
        mx = jnp.maximum(jnp.maximum(l0, l1), l2)
        e0, e1, e2 = jnp.exp(l0 - mx), jnp.exp(l1 - mx), jnp.exp(l2 - mx)
        num = e0 * out_s[0, rows, :] + e1 * out_s[1, rows, :] + e2 * out_s[2, rows, :]
        o_ref[0, rows, :] = (num / (e0 + e1 + e2)).astype(o_ref.dtype)
        return carry

    lax.fori_loop(0, SEQ // chunk, combine, 0)


def _attention(qkv, chunk=256, chains=8):
    b, s, _ = qkv.shape
    assert s == SEQ

    def head_block(part, group):
        col0 = part * ATTN_HEADS + group * HEADS_PER_GROUP
        return pl.BlockSpec((1, s, HEAD_DIM), lambda bi, h: (bi, 0, col0 + h))

    in_specs = [head_block(part, group) for group in range(N_GROUPS) for part in range(3)]
    return pl.pallas_call(
        functools.partial(_attn_kernel, chunk=chunk, chains=chains),
        grid=(b, HEADS_PER_GROUP),
        in_specs=in_specs,
        out_specs=pl.BlockSpec((1, s, HEAD_DIM), lambda bi, h: (bi, 0, h)),
        out_shape=jax.ShapeDtypeStruct((b, s, GROUP_WIDTH), jnp.bfloat16),
        scratch_shapes=[pltpu.VMEM((s, HEAD_DIM), jnp.float32),
                        pltpu.VMEM((s, HEAD_DIM), jnp.float32),
                        pltpu.VMEM((s, HEAD_DIM), jnp.float32),
                        pltpu.VMEM((N_GROUPS, s, HEAD_DIM), jnp.float32),
                        pltpu.VMEM((N_GROUPS, s, HEAD_DIM), jnp.float32)],
        compiler_params=_params(("arbitrary", "arbitrary")),
        name="dilated_attn",
    )(*([qkv] * 9))


def _resid_layer_norm(x, f, g, b):
    z = DEEPNORM_ALPHA * x + f
    mu = jnp.mean(z, axis=-1, keepdims=True)
    zc = z - mu
    var = jnp.mean(zc * zc, axis=-1, keepdims=True)
    return zc * lax.rsqrt(var + LN_EPS) * g + b


def _pack_bf16_pairs(y):
    c = y.shape[1] // 2
    bits = pltpu.bitcast(y.astype(jnp.bfloat16).astype(jnp.float32), jnp.uint32)
    return bits[:, :c] | (bits[:, c:] >> 16)


def _unpack_bf16_pairs(w):
    hi = pltpu.bitcast(w & jnp.uint32(0xFFFF0000), jnp.float32).astype(jnp.bfloat16)
    lo = pltpu.bitcast(w << 16, jnp.float32).astype(jnp.bfloat16)
    return hi, lo


def _mixer_tail_kernel(cb_ref, cc_ref, ch_ref, cch_ref, chh_ref, pu_ref, puh_ref, attn_ref,
                       g0a_ref, g0b_ref, g1a_ref, g1b_ref, g2a_ref, g2b_ref, x_ref,
                       convw_ref, pscale_ref, lng_ref, lnb_ref,
                       wconv_ref, wattn_ref, wpool_ref, pmix_ref, wo_ref, o_ref, obf_ref,
                       *, tm, tiles_per_seq, packed):
    seq_tile = pl.program_id(0) % tiles_per_seq
    not_first = (seq_tile != 0).astype(jnp.float32)
    f32 = lambda ref: ref[...].astype(jnp.float32)
    gate = lambda lo_ref, hi_ref: jax.nn.sigmoid(jnp.concatenate([f32(lo_ref), f32(hi_ref)], axis=1))

    u = f32(cc_ref) * f32(ch_ref)
    uh = f32(cch_ref) * f32(chh_ref) * not_first
    ext = jnp.concatenate([uh, u], axis=0)
    u1 = pltpu.roll(ext, 1, 0)[CONV_HALO:]
    u2 = pltpu.roll(ext, 2, 0)[CONV_HALO:]
    cw = convw_ref[...]
    conv = u2 * cw[0:1] + u1 * cw[1:2]
    conv = conv + u * cw[2:3]
    ya_in = (f32(cb_ref) * conv).astype(jnp.bfloat16)
    y_a = jnp.dot(ya_in, wconv_ref[...], preferred_element_type=jnp.float32)
    merged = gate(g0a_ref, g0b_ref) * y_a

    y_b = jnp.dot(attn_ref[...], wattn_ref[...], preferred_element_type=jnp.float32)
    merged = merged + gate(g1a_ref, g1b_ref) * y_b

    pu = f32(pu_ref)
    pext = jnp.concatenate([f32(puh_ref) * not_first, pu], axis=0)
    pos = seq_tile * tm + lax.broadcasted_iota(jnp.int32, (tm, 1), 0) + 1
    pscale = pscale_ref[...]
    y_c = None
    for gi, w in enumerate(POOL_WINDOWS):
        cols = slice(gi * POOL_GROUP_DIM, (gi + 1) * POOL_GROUP_DIM)
        run = pext[:, cols]
        step = 1
        while step < w:
            run = run + pltpu.roll(run, step, 0)
            step *= 2
        inv_cnt = 1.0 / jnp.minimum(pos, w).astype(jnp.float32)
        pooled = run[POOL_HALO:] * inv_cnt - pu[:, cols]
        mixed = jnp.dot(pooled.astype(jnp.bfloat16), pmix_ref[gi], preferred_element_type=jnp.float32)
        mixed = (mixed * pscale[:, cols]).astype(jnp.bfloat16)
        part = jnp.dot(mixed, wpool_ref[cols, :], preferred_element_type=jnp.float32)
        y_c = part if y_c is None else y_c + part
    merged = merged + gate(g2a_ref, g2b_ref) * y_c

    f = jnp.dot(merged.astype(jnp.bfloat16), wo_ref[...], preferred_element_type=jnp.float32)
    y = _resid_layer_norm(x_ref[...], f, lng_ref[...], lnb_ref[...])
    o_ref[...] = y
    obf_ref[...] = _pack_bf16_pairs(y) if packed else y.astype(jnp.bfloat16)


def _mixer_tail(conv_p, pool_gates_p, attn, x, conv_w, pool_scale, ln_g, ln_b,
                w_conv_bf, w_attn_bf, w_pool_bf, pool_mix_bf, w_o_bf, packed=False, tm=256):
    m = conv_p.shape[0]
    gate_blocks = N_GROUPS * D_MODEL // CONV_WIDTH
    tiles_per_seq = SEQ // tm
    row = lambda width, col: pl.BlockSpec((tm, width), lambda i: (i, col))

    def halo(rows, col):
        per_tile = tm // rows
        return pl.BlockSpec((rows, CONV_WIDTH), lambda i: (jnp.maximum(i * per_tile - 1, 0), col))

    full = lambda shape: pl.BlockSpec(shape, lambda i: (0,) * len(shape), pipeline_mode=pl.Buffered(1))
    in_specs = [row(CONV_WIDTH, 0), row(CONV_WIDTH, 1), row(CONV_WIDTH, 2),
                halo(CONV_HALO, 1), halo(CONV_HALO, 2),
                row(POOL_WIDTH, 0), halo(POOL_HALO, 0),
                row(GROUP_WIDTH, 0)]
    in_specs += [row(CONV_WIDTH, 1 + gb) for gb in range(gate_blocks)]
    in_specs += [row(D_MODEL, 0),
                 full((CONV_KERNEL, CONV_WIDTH)), full((1, POOL_WIDTH)), full((1, D_MODEL)), full((1, D_MODEL)),
                 full((CONV_WIDTH, D_MODEL)), full((GROUP_WIDTH, D_MODEL)), full((POOL_WIDTH, D_MODEL)),
                 full((len(POOL_WINDOWS), POOL_GROUP_DIM, POOL_GROUP_DIM)), full((D_MODEL, D_MODEL))]
    return pl.pallas_call(
        functools.partial(_mixer_tail_kernel, tm=tm, tiles_per_seq=tiles_per_seq, packed=packed),
        grid=(m // tm,),
        in_specs=in_specs,
        out_specs=[row(D_MODEL, 0), row(D_MODEL // 2 if packed else D_MODEL, 0)],
        out_shape=[jax.ShapeDtypeStruct((m, D_MODEL), jnp.float32),
                   jax.ShapeDtypeStruct((m, D_MODEL // 2), jnp.uint32) if packed
                   else jax.ShapeDtypeStruct((m, D_MODEL), jnp.bfloat16)],
        compiler_params=_params(("arbitrary",)),
        name="mixer_tail",
    )(conv_p, conv_p, conv_p, conv_p, conv_p, pool_gates_p, pool_gates_p, attn,
      *([pool_gates_p] * gate_blocks), x,
      conv_w, pool_scale.reshape(1, POOL_WIDTH), ln_g.reshape(1, D_MODEL), ln_b.reshape(1, D_MODEL),
      w_conv_bf, w_attn_bf, w_pool_bf, pool_mix_bf, w_o_bf)


def _route_tile(y, w_ref, idx_ref, gate_ref, rank_ref, cnt_ref, carry_ref, *, tm):
    @pl.when(pl.program_id(0) == 0)
    def _():
        carry_ref[...] = jnp.zeros_like(carry_ref)

    w = w_ref[...]
    y_hi, w_hi = y.astype(jnp.bfloat16), w.astype(jnp.bfloat16)
    y_lo = (y - y_hi.astype(jnp.float32)).astype(jnp.bfloat16)
    w_lo = (w - w_hi.astype(jnp.float32)).astype(jnp.bfloat16)
    logits = (jnp.dot(y_hi, w_hi, preferred_element_type=jnp.float32)
              + jnp.dot(y_lo, w_hi, preferred_element_type=jnp.float32)
              + jnp.dot(y_hi, w_lo, preferred_element_type=jnp.float32))
    lane = lax.broadcasted_iota(jnp.int32, logits.shape, 1)
    logits = jnp.where(lane < N_EXPERTS, logits, -jnp.inf)
    m1 = jnp.max(logits, axis=-1, keepdims=True)
    i1 = jnp.min(jnp.where(logits == m1, lane, LANES), axis=-1, keepdims=True)
    rest = jnp.where(lane == i1, -jnp.inf, logits)
    m2 = jnp.max(rest, axis=-1, keepdims=True)
    i2 = jnp.min(jnp.where(rest == m2, lane, LANES), axis=-1, keepdims=True)
    e = jnp.exp(m2 - m1)
    den = 1.0 + e
    idx_ref[:, 0:1] = i1
    idx_ref[:, 1:2] = i2
    gate_ref[:, 0:1] = 1.0 / den
    gate_ref[:, 1:2] = e / den

    oh1 = (lane == i1).astype(jnp.float32)
    oh2 = (lane == i2).astype(jnp.float32)
    both = oh1 + oh2
    r = lax.broadcasted_iota(jnp.int32, (tm, tm), 0)
    c = lax.broadcasted_iota(jnp.int32, (tm, tm), 1)
    strict_lower = (r > c).astype(jnp.bfloat16)
    before = jnp.dot(strict_lower, both.astype(jnp.bfloat16), preferred_element_type=jnp.float32)
    before = before + carry_ref[...]
    rank_ref[:, 0:1] = jnp.sum(oh1 * before, axis=-1, keepdims=True).astype(jnp.int32)
    rank_ref[:, 1:2] = jnp.sum(oh2 * before, axis=-1, keepdims=True).astype(jnp.int32)
    carry_ref[...] += jnp.sum(both, axis=0, keepdims=True)
    cnt_ref[...] = carry_ref[...]


def _route_kernel(x_ref, w_ref, idx_ref, gate_ref, rank_ref, cnt_ref, carry_ref, *, tm):
    _route_tile(x_ref[...], w_ref, idx_ref, gate_ref, rank_ref, cnt_ref, carry_ref, tm=tm)


def _route(x, w_router, tm=512):
    m = x.shape[0]
    w_pad = jnp.pad(w_router, ((0, 0), (0, LANES - N_EXPERTS)))
    pair = pl.BlockSpec((tm, TOP_K), lambda i: (i, 0))
    return pl.pallas_call(
        functools.partial(_route_kernel, tm=tm),
        grid=(m // tm,),
        in_specs=[pl.BlockSpec((tm, D_MODEL), lambda i: (i, 0)),
                  pl.BlockSpec((D_MODEL, LANES), lambda i: (0, 0))],
        out_specs=[pair, pair, pair, pl.BlockSpec((1, LANES), lambda i: (0, 0))],
        out_shape=[jax.ShapeDtypeStruct((m, TOP_K), jnp.int32),
                   jax.ShapeDtypeStruct((m, TOP_K), jnp.float32),
                   jax.ShapeDtypeStruct((m, TOP_K), jnp.int32),
                   jax.ShapeDtypeStruct((1, LANES), jnp.float32)],
        scratch_shapes=[pltpu.VMEM((1, LANES), jnp.float32)],
        compiler_params=_params(("arbitrary",)),
        name="router",
    )(x, w_pad)


def _swiglu_kernel(blk_exp_ref, blk_rows_ref, n_used_ref, x_ref, wg_ref, wu_ref, wd_ref, *refs,
                   tm, sub, last_j, resid_ln):
    del blk_exp_ref, n_used_ref
    i, j = pl.program_id(0), pl.program_id(1)
    if resid_ln:
        res_hbm, g_ref, b_ref, o_ref, obf_ref, res_ref, res_sem = refs
        res_copy = pltpu.make_async_copy(res_hbm.at[pl.ds(pl.multiple_of(i * tm, tm), tm)], res_ref, res_sem)
        xbf_ref = x_ref
    else:
        o_ref, xbf_ref = refs
    n_sub = (blk_rows_ref[i] + sub - 1) // sub

    @pl.when(j == 0)
    def _():
        if resid_ln:
            res_copy.start()
        else:
            half = x_ref.shape[1]
            xbf_ref[:, :half], xbf_ref[:, half:] = _unpack_bf16_pairs(x_ref[...])
        o_ref[...] = jnp.zeros_like(o_ref)

    for v in range(1, tm // sub + 1):
        rows = v * sub

        @pl.when(n_sub == v)
        def _(rows=rows):
            x = xbf_ref[0:rows, :]
            gate = jnp.dot(x, wg_ref[0].astype(jnp.bfloat16), preferred_element_type=jnp.float32)
            up = jnp.dot(x, wu_ref[0].astype(jnp.bfloat16), preferred_element_type=jnp.float32)
            h = (gate * jax.nn.sigmoid(gate) * up).astype(jnp.bfloat16)
            o_ref[0:rows, :] += jnp.dot(h, wd_ref[0].astype(jnp.bfloat16),
                                        preferred_element_type=jnp.float32)

    if resid_ln:
        @pl.when(j == last_j)
        def _():
            res_copy.wait()

            def norm_rows(ci, carry):
                rows = pl.ds(pl.multiple_of(ci * sub, sub), sub)
                y = _resid_layer_norm(res_ref[rows, :], o_ref[rows, :], g_ref[...], b_ref[...])
                o_ref[rows, :] = y
                obf_ref[rows, :] = y.astype(jnp.bfloat16)
                return carry

            lax.fori_loop(0, tm // sub, norm_rows, 0)


def _swiglu_blocks(x, w_gate, w_up, w_down, blk_exp, blk_rows, n_used, tm, tf, sub=256, resid_ln=None):
    r = x.shape[0]
    nf = D_FF // tf
    last = nf - 1

    def live(i, nu):
        return jnp.maximum(jnp.minimum(i, nu[0] - 1), 0)

    def jj(i, j, nu):
        return jnp.where(i < nu[0], j, last)

    once = pl.Buffered(1)
    x_spec = pl.BlockSpec((tm, x.shape[1]), lambda i, j, be, br, nu: (live(i, nu), 0),
                          pipeline_mode=once if resid_ln is not None else None)
    row_out = pl.BlockSpec((tm, D_MODEL), lambda i, j, be, br, nu: (i, 0), pipeline_mode=once)
    vec = pl.BlockSpec((1, D_MODEL), lambda i, j, be, br, nu: (0, 0))
    in_specs = [x_spec,
                pl.BlockSpec((1, D_MODEL, tf), lambda i, j, be, br, nu: (be[live(i, nu)], 0, jj(i, j, nu))),
                pl.BlockSpec((1, D_MODEL, tf), lambda i, j, be, br, nu: (be[live(i, nu)], 0, jj(i, j, nu))),
                pl.BlockSpec((1, tf, D_MODEL), lambda i, j, be, br, nu: (be[live(i, nu)], jj(i, j, nu), 0))]
    args = [x, w_gate, w_up, w_down]
    out_specs = row_out
    out_shape = jax.ShapeDtypeStruct((r, D_MODEL), jnp.float32)
    if resid_ln is not None:
        assert x.dtype == jnp.bfloat16
        res, g, b = resid_ln
        in_specs += [pl.BlockSpec(memory_space=pl.ANY), vec, vec]
        args += [res, g.reshape(1, D_MODEL), b.reshape(1, D_MODEL)]
        out_specs = [row_out, row_out]
        out_shape = [out_shape, jax.ShapeDtypeStruct((r, D_MODEL), jnp.bfloat16)]
        scratch = [pltpu.VMEM((tm, D_MODEL), jnp.float32), pltpu.SemaphoreType.DMA(())]
    else:
        assert x.dtype == jnp.uint32
        scratch = [pltpu.VMEM((tm, D_MODEL), jnp.bfloat16)]
    grid_spec = pltpu.PrefetchScalarGridSpec(
        num_scalar_prefetch=3, grid=(r // tm, nf),
        in_specs=in_specs, out_specs=out_specs, scratch_shapes=scratch)
    return pl.pallas_call(
        functools.partial(_swiglu_kernel, tm=tm, sub=sub, last_j=last, resid_ln=resid_ln is not None),
        grid_spec=grid_spec,
        out_shape=out_shape,
        compiler_params=_params(("arbitrary", "arbitrary")),
        name="swiglu_blocks",
    )(blk_exp, blk_rows, n_used, *args)


ZERO_ROWS = 256


def _dispatch_kernel(zero_slab_ref, dest_ref, x_ref, xs_ref, zero_ref, sem, *, tm):
    @pl.when(pl.program_id(0) == 0)
    def _():
        zero_ref[...] = jnp.zeros_like(zero_ref)
        slabs = [pltpu.make_async_copy(zero_ref, xs_ref.at[pl.ds(z * ZERO_ROWS, ZERO_ROWS)], sem)
                 for z in range(zero_slab_ref.shape[0])]
        for z, slab in enumerate(slabs):
            pl.when(zero_slab_ref[z] != 0)(slab.start)
        for z, slab in enumerate(slabs):
            pl.when(zero_slab_ref[z] != 0)(slab.wait)

    def issue(t, carry):
        for k in range(TOP_K):
            pltpu.make_async_copy(x_ref.at[pl.ds(t, 1)],
                                  xs_ref.at[pl.ds(dest_ref[TOP_K * t + k], 1)], sem).start()
        return carry

    lax.fori_loop(0, tm, issue, 0, unroll=8)
    for k in range(TOP_K):
        pltpu.make_async_copy(x_ref, xs_ref.at[pl.ds(0, tm)], sem).wait()


def _dispatch(x, dest_flat, zero_slab, rows, tm=256):
    m, width = x.shape
    grid_spec = pltpu.PrefetchScalarGridSpec(
        num_scalar_prefetch=1, grid=(m // tm,),
        in_specs=[pl.BlockSpec((TOP_K * tm,), lambda i, zs: (i,), memory_space=pltpu.SMEM),
                  pl.BlockSpec((tm, width), lambda i, zs: (i, 0))],
        out_specs=pl.BlockSpec(memory_space=pl.ANY),
        scratch_shapes=[pltpu.VMEM((ZERO_ROWS, width), x.dtype), pltpu.SemaphoreType.DMA(())])
    return pl.pallas_call(
        functools.partial(_dispatch_kernel, tm=tm),
        grid_spec=grid_spec,
        out_shape=jax.ShapeDtypeStruct((rows, width), x.dtype),
        compiler_params=_params(("arbitrary",)),
        name="dispatch",
    )(zero_slab, dest_flat, x)


def _combine_ln_kernel(dest_ref, dest_next_ref, x_ref, gate_ref, g_ref, b_ref, y_ref, o_ref,
                       ybuf_ref, sems, *, tm):
    i, n = pl.program_id(0), pl.num_programs(0)
    slot = lax.rem(i, 2)

    def gather(d_ref, s):
        def issue(t, carry):
            for k in range(TOP_K):
                pltpu.make_async_copy(y_ref.at[pl.ds(d_ref[TOP_K * t + k], 1)],
                                      ybuf_ref.at[s, k, pl.ds(t, 1)], sems.at[s]).start()
            return carry

        lax.fori_loop(0, tm, issue, 0, unroll=8)

    @pl.when(i == 0)
    def _():
        gather(dest_ref, 0)

    @pl.when(i + 1 < n)
    def _():
        gather(dest_next_ref, 1 - slot)

    for k in range(TOP_K):
        pltpu.make_async_copy(y_ref.at[pl.ds(0, tm)], ybuf_ref.at[slot, k], sems.at[slot]).wait()
    gate = gate_ref[...]
    f = ybuf_ref[slot, 0] * gate[:, 0:1] + ybuf_ref[slot, 1] * gate[:, 1:2]
    o_ref[...] = _resid_layer_norm(x_ref[...], f, g_ref[...], b_ref[...])


def _combine_ln(x, y, dest_flat, gates, g, b, tm=256):
    m = x.shape[0]
    n_tiles = m // tm
    row = pl.BlockSpec((tm, D_MODEL), lambda i: (i, 0))
    vec = pl.BlockSpec((1, D_MODEL), lambda i: (0, 0))
    return pl.pallas_call(
        functools.partial(_combine_ln_kernel, tm=tm),
        grid=(n_tiles,),
        in_specs=[pl.BlockSpec((TOP_K * tm,), lambda i: (i,), memory_space=pltpu.SMEM),
                  pl.BlockSpec((TOP_K * tm,), lambda i: (jnp.minimum(i + 1, n_tiles - 1),),
                               memory_space=pltpu.SMEM),
                  row, pl.BlockSpec((tm, TOP_K), lambda i: (i, 0)), vec, vec,
                  pl.BlockSpec(memory_space=pl.ANY)],
        out_specs=row,
        out_shape=jax.ShapeDtypeStruct((m, D_MODEL), jnp.float32),
        scratch_shapes=[pltpu.VMEM((2, TOP_K, tm, D_MODEL), jnp.float32),
                        pltpu.SemaphoreType.DMA((2,))],
        compiler_params=_params(("arbitrary",)),
        name="combine_ln",
    )(dest_flat, dest_flat, x, gates, g.reshape(1, D_MODEL), b.reshape(1, D_MODEL), y)


def _rope_tables(seq):
    inv_freq = ROPE_THETA ** (-jnp.arange(0, ROPE_DIM, 2, dtype=jnp.float32) / ROPE_DIM)
    ang = jnp.arange(seq, dtype=jnp.float32)[:, None] * inv_freq[None, :]
    cos, sin = jnp.cos(ang), jnp.sin(ang)
    pad = HEAD_DIM - ROPE_DIM
    cos_full = jnp.concatenate([cos, cos, jnp.ones((seq, pad), jnp.float32)], axis=1)
    sin_full = jnp.concatenate([-sin, sin, jnp.zeros((seq, pad), jnp.float32)], axis=1)
    return cos_full, sin_full


def _mixer(x, x_bf, batch, w_in, conv_w, w_br_conv, w_br_attn, pool_mix, pool_scale, w_br_pool, w_o,
           ln_g, ln_b, rope_cos, rope_sin, packed=False):
    conv_cols = CONV_KERNEL * CONV_WIDTH
    pool_col = conv_cols + QKV_WIDTH
    conv_p = _project(x_bf, w_in, 0, conv_cols, 1024, jnp.bfloat16)
    qkv = _project(x_bf, w_in, conv_cols, QKV_WIDTH, QKV_WIDTH // 3, jnp.bfloat16,
                   rope=(rope_cos, rope_sin), rope_tiles=2)
    pool_gates_p = _project(x_bf, w_in, pool_col, POOL_WIDTH + N_GROUPS * D_MODEL, 1024, jnp.bfloat16)
    attn = _attention(qkv.reshape(batch, SEQ, QKV_WIDTH))
    attn = attn.reshape(batch * SEQ, GROUP_WIDTH)
    bf = lambda w: w.astype(jnp.bfloat16)
    return _mixer_tail(conv_p, pool_gates_p, attn, x, conv_w, pool_scale, ln_g, ln_b,
                       bf(w_br_conv), bf(w_br_attn), bf(w_br_pool), bf(pool_mix), bf(w_o), packed=packed)


def _dense_ffn(x, x_bf, w_gate, w_up, w_down, ln_g, ln_b, tm=1024, tf=512):
    n_blk = x.shape[0] // tm
    blk_exp = jnp.zeros((n_blk,), jnp.int32)
    blk_rows = jnp.full((n_blk,), tm, jnp.int32)
    n_used = jnp.full((1,), n_blk, jnp.int32)
    return _swiglu_blocks(x_bf, w_gate[None], w_up[None], w_down[None], blk_exp, blk_rows, n_used, tm, tf,
                          resid_ln=(x, ln_g, ln_b))


def _moe_ffn(x, x_packed, w_router, w_gate, w_up, w_down, ln_g, ln_b, tm=1024, tf=512):
    m = x.shape[0]
    idx, gates, rank, cnt = _route(x, w_router)
    counts = cnt[0, :N_EXPERTS].astype(jnp.int32)
    padded = (counts + tm - 1) // tm * tm
    pend = jnp.cumsum(padded)
    pstart = pend - padded
    dest_flat = (pstart[idx] + rank).reshape(-1)
    n_blk = m * TOP_K // tm + N_EXPERTS
    blk_start = jnp.arange(n_blk, dtype=jnp.int32) * tm
    blk_exp = jnp.minimum(jnp.sum(blk_start[:, None] >= pend[None, :], axis=1), N_EXPERTS - 1).astype(jnp.int32)
    blk_rows = jnp.clip(pstart[blk_exp] + counts[blk_exp] - blk_start, 0, tm).astype(jnp.int32)
    n_used = (pend[-1:] // tm).astype(jnp.int32)
    blk_rows = jnp.where(jnp.arange(n_blk) < n_used[0], blk_rows, 0)
    slab_end = (jnp.arange(n_blk * tm // ZERO_ROWS, dtype=jnp.int32) + 1) * ZERO_ROWS
    slab_exp = jnp.repeat(blk_exp, tm // ZERO_ROWS)
    zero_slab = (slab_end > pstart[slab_exp] + counts[slab_exp]).astype(jnp.int32)
    xs = _dispatch(x_packed, dest_flat, zero_slab, n_blk * tm)
    ys = _swiglu_blocks(xs, w_gate, w_up, w_down, blk_exp, blk_rows, n_used, tm, tf)
    return _combine_ln(x, ys, dest_flat, gates, ln_g, ln_b)


def kernel(x, l0_w_in, l0_conv_w, l0_w_br_conv, l0_w_br_attn, l0_pool_mix, l0_pool_scale, l0_w_br_pool, l0_w_o, l0_ln1_g, l0_ln1_b, l0_ffn_gate, l0_ffn_up, l0_ffn_down, l0_ln2_g, l0_ln2_b, l1_w_in, l1_conv_w, l1_w_br_conv, l1_w_br_attn, l1_pool_mix, l1_pool_scale, l1_w_br_pool, l1_w_o, l1_ln1_g, l1_ln1_b, l1_router, l1_exp_gate, l1_exp_up, l1_exp_down, l1_ln2_g, l1_ln2_b):
    batch, seq, d = x.shape
    assert seq == SEQ and d == D_MODEL
    rope_cos, rope_sin = _rope_tables(seq)
    x0 = x.reshape(batch * seq, d)
    x1, x1_bf = _mixer(x0, x0.astype(jnp.bfloat16), batch, l0_w_in, l0_conv_w, l0_w_br_conv, l0_w_br_attn,
                       l0_pool_mix, l0_pool_scale, l0_w_br_pool, l0_w_o, l0_ln1_g, l0_ln1_b, rope_cos, rope_sin)
    x2, x2_bf = _dense_ffn(x1, x1_bf, l0_ffn_gate, l0_ffn_up, l0_ffn_down, l0_ln2_g, l0_ln2_b)
    x3, x3_packed = _mixer(x2, x2_bf, batch, l1_w_in, l1_conv_w, l1_w_br_conv, l1_w_br_attn,
                           l1_pool_mix, l1_pool_scale, l1_w_br_pool, l1_w_o, l1_ln1_g, l1_ln1_b,
                           rope_cos, rope_sin, packed=True)
    x4 = _moe_ffn(x3, x3_packed, l1_router, l1_exp_gate, l1_exp_up, l1_exp_down, l1_ln2_g, l1_ln2_b)
    return x4.reshape(batch, seq, d)
```

```python
import functools

import jax
import jax.numpy as jnp
from jax import lax
from jax.experimental import pallas as pl
from jax.experimental.pallas import tpu as pltpu

D_MODEL = 2048
SEQ = 4096
CONV_WIDTH = D_MODEL // 2
CONV_KERNEL = 3
HEAD_DIM = 128
HEADS_PER_GROUP = 4
GROUP_WIDTH = HEADS_PER_GROUP * HEAD_DIM
ATTN_PATTERNS = ((128, 1), (512, 4), (2048, 16))
N_GROUPS = len(ATTN_PATTERNS)
ATTN_HEADS = N_GROUPS * HEADS_PER_GROUP
QKV_WIDTH = 3 * N_GROUPS * GROUP_WIDTH
ATTN_BLOCK = 128
ROPE_THETA = 500000.0
ROPE_DIM = HEAD_DIM // 4
POOL_WINDOWS = (2, 4, 8, 16)
POOL_WIDTH = D_MODEL // 2
POOL_GROUP_DIM = POOL_WIDTH // len(POOL_WINDOWS)
POOL_HALO = 16
CONV_HALO = 16
D_FF = 7 * D_MODEL // 2
N_EXPERTS = 8
TOP_K = 2
LN_EPS = 1e-5
DEPTH = 2
DEEPNORM_ALPHA = (2.0 * DEPTH) ** 0.25
NEG_INF = -1e30
LANES = 128

VMEM_LIMIT = 56 * 1024 * 1024


def _params(sem, vmem=VMEM_LIMIT):
    return pltpu.CompilerParams(dimension_semantics=sem, vmem_limit_bytes=vmem)


W_SLAB = 512


def _rope(t, cos, sin):
    half = ROPE_DIM // 2
    lane = lax.broadcasted_iota(jnp.int32, t.shape, 1)
    partner = jnp.where(lane < half, pltpu.roll(t, HEAD_DIM - half, 1), pltpu.roll(t, half, 1))
    return t * cos + partner * sin


def _proj_kernel(*refs, n_slabs, rope_tiles):
    x_ref, w_refs = refs[0], refs[1:1 + n_slabs]
    o_ref, wbf_ref = refs[-2], refs[-1]

    @pl.when(pl.program_id(1) == 0)
    def _():
        for t, w_ref in enumerate(w_refs):
            wbf_ref[:, t * W_SLAB:(t + 1) * W_SLAB] = w_ref[...].astype(jnp.bfloat16)

    def plain():
        o_ref[...] = jnp.dot(x_ref[...], wbf_ref[...],
                             preferred_element_type=jnp.float32).astype(o_ref.dtype)

    if rope_tiles == 0:
        plain()
        return
    cos_ref, sin_ref = refs[1 + n_slabs], refs[2 + n_slabs]
    is_rope_tile = pl.program_id(0) < rope_tiles

    @pl.when(is_rope_tile)
    def _():
        cos, sin = cos_ref[...], sin_ref[...]
        for t in range(n_slabs):
            acc = jnp.dot(x_ref[...], wbf_ref[:, t * W_SLAB:(t + 1) * W_SLAB],
                          preferred_element_type=jnp.float32)
            for h in range(W_SLAB // HEAD_DIM):
                cols = slice(h * HEAD_DIM, (h + 1) * HEAD_DIM)
                out_cols = slice(t * W_SLAB + h * HEAD_DIM, t * W_SLAB + (h + 1) * HEAD_DIM)
                o_ref[:, out_cols] = _rope(acc[:, cols], cos, sin).astype(o_ref.dtype)

    pl.when(jnp.logical_not(is_rope_tile))(plain)


def _project(x_bf, w, col0, width, tn, out_dtype, tm=1024, rope=None, rope_tiles=0):
    m, k = x_bf.shape
    assert col0 % W_SLAB == 0 and tn % W_SLAB == 0 and width % tn == 0 and m % tm == 0
    n_slabs = tn // W_SLAB
    slab0 = col0 // W_SLAB
    w_specs = [pl.BlockSpec((k, W_SLAB), lambda j, i, t=t: (0, slab0 + j * n_slabs + t))
               for t in range(n_slabs)]
    rope_specs, rope_args = [], []
    if rope_tiles:
        tiles_per_seq = SEQ // tm
        rope_specs = [pl.BlockSpec((tm, LANES), lambda j, i: (i % tiles_per_seq, 0))] * 2
        rope_args = list(rope)
    return pl.pallas_call(
        functools.partial(_proj_kernel, n_slabs=n_slabs, rope_tiles=rope_tiles),
        grid=(width // tn, m // tm),
        in_specs=[pl.BlockSpec((tm, k), lambda j, i: (i, 0))] + w_specs + rope_specs,
        out_specs=pl.BlockSpec((tm, tn), lambda j, i: (i, j)),
        out_shape=jax.ShapeDtypeStruct((m, width), out_dtype),
        scratch_shapes=[pltpu.VMEM((k, tn), jnp.bfloat16)],
        compiler_params=_params(("arbitrary", "arbitrary")),
        name="in_proj",
    )(x_bf, *([w] * n_slabs), *rope_args)


def _strided(start, size, stride):
    return pl.ds(start, size) if stride == 1 else pl.ds(start, size, stride=stride)


def _attn_kernel(q0_ref, k0_ref, v0_ref, q1_ref, k1_ref, v1_ref, q2_ref, k2_ref, v2_ref,
                 o_ref, q_s, k_s, v_s, out_s, lse_s, *, chunk, chains):
    groups = ((q0_ref, k0_ref, v0_ref), (q1_ref, k1_ref, v1_ref), (q2_ref, k2_ref, v2_ref))
    n_keys = 2 * ATTN_BLOCK
    a = lax.broadcasted_iota(jnp.int32, (ATTN_BLOCK, n_keys), 0)
    kk = lax.broadcasted_iota(jnp.int32, (ATTN_BLOCK, n_keys), 1)
    scale = HEAD_DIM ** -0.5

    for g, (q_ref, k_ref, v_ref) in enumerate(groups):
        d = ATTN_PATTERNS[g][1]
        if d > 1:
            def stage(ci, carry, q_ref=q_ref, k_ref=k_ref, v_ref=v_ref):
                rows = pl.ds(pl.multiple_of(ci * chunk, chunk), chunk)
                q_s[rows, :] = q_ref[0, rows, :].astype(jnp.float32)
                k_s[rows, :] = k_ref[0, rows, :].astype(jnp.float32)
                v_s[rows, :] = v_ref[0, rows, :].astype(jnp.float32)
                return carry

            lax.fori_loop(0, SEQ // chunk, stage, 0)
            q_src, k_src, v_src = q_s, k_s, v_s
        else:
            q_src, k_src, v_src = q_ref.at[0], k_ref.at[0], v_ref.at[0]

        blocks_per_residue = SEQ // d // ATTN_BLOCK
        unroll = min(chains, blocks_per_residue)
        residues_per_body = chains // unroll
        for c0 in range(0, d, residues_per_body):
            def block(blk, carry, c0=c0, d=d, g=g, residues_per_body=residues_per_body,
                      q_src=q_src, k_src=k_src, v_src=v_src):
                base = pl.multiple_of(blk * (ATTN_BLOCK * d), ATTN_BLOCK)
                key_base = pl.multiple_of(jnp.maximum(base - ATTN_BLOCK * d, 0), ATTN_BLOCK)
                self_key = jnp.where(blk == 0, 0, ATTN_BLOCK)
                valid = (kk >= a + self_key - ATTN_BLOCK) & (kk <= a + self_key)
                for c in range(c0, c0 + residues_per_body):
                    qb = q_src[_strided(base + c, ATTN_BLOCK, d), :].astype(jnp.bfloat16)
                    kb = k_src[_strided(key_base + c, n_keys, d), :].astype(jnp.bfloat16)
                    vb = v_src[_strided(key_base + c, n_keys, d), :].astype(jnp.bfloat16)
                    s = lax.dot_general(qb, kb, (((1,), (1,)), ((), ())),
                                        preferred_element_type=jnp.float32) * scale
                    s = jnp.where(valid, s, NEG_INF)
                    m = jnp.max(s, axis=-1, keepdims=True)
                    p = jnp.exp(s - m)
                    l = jnp.sum(p, axis=-1, keepdims=True)
                    o = jnp.dot(p.astype(jnp.bfloat16), vb, preferred_element_type=jnp.float32)
                    rows = _strided(base + c, ATTN_BLOCK, d)
                    out_s[g, rows, :] = o / l
                    lse_s[g, rows, :] = jnp.broadcast_to(m + jnp.log(l), (ATTN_BLOCK, HEAD_DIM))
                return carry

            lax.fori_loop(0, blocks_per_residue, block, 0, unroll=unroll)

    def combine(ci, carry):
        rows = pl.ds(pl.multiple_of(ci * chunk, chunk), chunk)
        l0, l1, l2 = lse_s[0, rows, :], lse_s[1, rows, :], lse_s[2, rows, :]
        mx = jnp.maximum(jnp.maximum(l0, l1), l2)
        e0, e1, e2 = jnp.exp(l0 - mx), jnp.exp(l1 - mx), jnp.exp(l2 - mx)
        num = e0 * out_s[0, rows, :] + e1 * out_s[1, rows, :] + e2 * out_s[2, rows, :]
        o_ref[0, rows, :] = (num / (e0 + e1 + e2)).astype(o_ref.dtype)
        return carry

    lax.fori_loop(0, SEQ // chunk, combine, 0)


def _attention(qkv, chunk=256, chains=8):
    b, s, _ = qkv.shape
    assert s == SEQ

    def head_block(part, group):
        col0 = part * ATTN_HEADS + group * HEADS_PER_GROUP
        return pl.BlockSpec((1, s, HEAD_DIM), lambda bi, h: (bi, 0, col0 + h))

    in_specs = [head_block(part, group) for group in range(N_GROUPS) for part in range(3)]
    return pl.pallas_call(
        functools.partial(_attn_kernel, chunk=chunk, chains=chains),
        grid=(b, HEADS_PER_GROUP),
        in_specs=in_specs,
        out_specs=pl.BlockSpec((1, s, HEAD_DIM), lambda bi, h: (bi, 0, h)),
        out_shape=jax.ShapeDtypeStruct((b, s, GROUP_WIDTH), jnp.bfloat16),
        scratch_shapes=[pltpu.VMEM((s, HEAD_DIM), jnp.float32),
                        pltpu.VMEM((s, HEAD_DIM), jnp.float32),
                        pltpu.VMEM((s, HEAD_DIM), jnp.float32),
                        pltpu.VMEM((N_GROUPS, s, HEAD_DIM), jnp.float32),
                        pltpu.VMEM((N_GROUPS, s, HEAD_DIM), jnp.float32)],
        compiler_params=_params(("arbitrary", "arbitrary")),
        name="dilated_attn",
    )(*([qkv] * 9))


def _resid_layer_norm(x, f, g, b):
    z = DEEPNORM_ALPHA * x + f
    mu = jnp.mean(z, axis=-1, keepdims=True)
    zc = z - mu
    var = jnp.mean(zc * zc, axis=-1, keepdims=True)
    return zc * lax.rsqrt(var + LN_EPS) * g + b


def _pack_bf16_pairs(y):
    c = y.shape[1] // 2
    bits = pltpu.bitcast(y.astype(jnp.bfloat16).astype(jnp.float32), jnp.uint32)
    return bits[:, :c] | (bits[:, c:] >> 16)


def _unpack_bf16_pairs(w):
    hi = pltpu.bitcast(w & jnp.uint32(0xFFFF0000), jnp.float32).astype(jnp.bfloat16)
    lo = pltpu.bitcast(w << 16, jnp.float32).astype(jnp.bfloat16)
    return hi, lo


def _mixer_tail_kernel(cb_ref, cc_ref, ch_ref, cch_ref, chh_ref, pu_ref, puh_ref, attn_ref,
                       g0a_ref, g0b_ref, g1a_ref, g1b_ref, g2a_ref, g2b_ref, x_ref,
                       convw_ref, pscale_ref, lng_ref, lnb_ref,
                       wconv_ref, wattn_ref, wpool_ref, pmix_ref, wo_ref, o_ref, obf_ref,
                       *, tm, tiles_per_seq, packed):
    seq_tile = pl.program_id(0) % tiles_per_seq
    not_first = (seq_tile != 0).astype(jnp.float32)
    f32 = lambda ref: ref[...].astype(jnp.float32)
    gate = lambda lo_ref, hi_ref: jax.nn.sigmoid(jnp.concatenate([f32(lo_ref), f32(hi_ref)], axis=1))

    u = f32(cc_ref) * f32(ch_ref)
    uh = f32(cch_ref) * f32(chh_ref) * not_first
    ext = jnp.concatenate([uh, u], axis=0)
    u1 = pltpu.roll(ext, 1, 0)[CONV_HALO:]
    u2 = pltpu.roll(ext, 2, 0)[CONV_HALO:]
    cw = convw_ref[...]
    conv = u2 * cw[0:1] + u1 * cw[1:2]
    conv = conv + u * cw[2:3]
    ya_in = (f32(cb_ref) * conv).astype(jnp.bfloat16)
    y_a = jnp.dot(ya_in, wconv_ref[...], preferred_element_type=jnp.float32)
    merged = gate(g0a_ref, g0b_ref) * y_a

    y_b = jnp.dot(attn_ref[...], wattn_ref[...], preferred_element_type=jnp.float32)
    merged = merged + gate(g1a_ref, g1b_ref) * y_b

    pu = f32(pu_ref)
    pext = jnp.concatenate([f32(puh_ref) * not_first, pu], axis=0)
    pos = seq_tile * tm + lax.broadcasted_iota(jnp.int32, (tm, 1), 0) + 1
    pscale = pscale_ref[...]
    y_c = None
    for gi, w in enumerate(POOL_WINDOWS):
        cols = slice(gi * POOL_GROUP_DIM, (gi + 1) * POOL_GROUP_DIM)
        run = pext[:, cols]
        step = 1
        while step < w:
            run = run + pltpu.roll(run, step, 0)
            step *= 2
        inv_cnt = 1.0 / jnp.minimum(pos, w).astype(jnp.float32)
        pooled = run[POOL_HALO:] * inv_cnt - pu[:, cols]
        mixed = jnp.dot(pooled.astype(jnp.bfloat16), pmix_ref[gi], preferred_element_type=jnp.float32)
        mixed = (mixed * pscale[:, cols]).astype(jnp.bfloat16)
        part = jnp.dot(mixed, wpool_ref[cols, :], preferred_element_type=jnp.float32)
        y_c = part if y_c is None else y_c + part
    merged = merged + gate(g2a_ref, g2b_ref) * y_c

    f = jnp.dot(merged.astype(jnp.bfloat16), wo_ref[...], preferred_element_type=jnp.float32)
    y = _resid_layer_norm(x_ref[...], f, lng_ref[...], lnb_ref[...])
    o_ref[...] = y
    obf_ref[...] = _pack_bf16_pairs(y) if packed else y.astype(jnp.bfloat16)


def _mixer_tail(conv_p, pool_gates_p, attn, x, conv_w, pool_scale, ln_g, ln_b,
                w_conv_bf, w_attn_bf, w_pool_bf, pool_mix_bf, w_o_bf, packed=False, tm=256):
    m = conv_p.shape[0]
    gate_blocks = N_GROUPS * D_MODEL // CONV_WIDTH
    tiles_per_seq = SEQ // tm
    row = lambda width, col: pl.BlockSpec((tm, width), lambda i: (i, col))

    def halo(rows, col):
        per_tile = tm // rows
        return pl.BlockSpec((rows, CONV_WIDTH), lambda i: (jnp.maximum(i * per_tile - 1, 0), col))

    full = lambda shape: pl.BlockSpec(shape, lambda i: (0,) * len(shape), pipeline_mode=pl.Buffered(1))
    in_specs = [row(CONV_WIDTH, 0), row(CONV_WIDTH, 1), row(CONV_WIDTH, 2),
                halo(CONV_HALO, 1), halo(CONV_HALO, 2),
                row(POOL_WIDTH, 0), halo(POOL_HALO, 0),
                row(GROUP_WIDTH, 0)]
    in_specs += [row(CONV_WIDTH, 1 + gb) for gb in range(gate_blocks)]
    in_specs += [row(D_MODEL, 0),
                 full((CONV_KERNEL, CONV_WIDTH)), full((1, POOL_WIDTH)), full((1, D_MODEL)), full((1, D_MODEL)),
                 full((CONV_WIDTH, D_MODEL)), full((GROUP_WIDTH, D_MODEL)), full((POOL_WIDTH, D_MODEL)),
                 full((len(POOL_WINDOWS), POOL_GROUP_DIM, POOL_GROUP_DIM)), full((D_MODEL, D_MODEL))]
    return pl.pallas_call(
        functools.partial(_mixer_tail_kernel, tm=tm, tiles_per_seq=tiles_per_seq, packed=packed),
        grid=(m // tm,),
        in_specs=in_specs,
        out_specs=[row(D_MODEL, 0), row(D_MODEL // 2 if packed else D_MODEL, 0)],
        out_shape=[jax.ShapeDtypeStruct((m, D_MODEL), jnp.float32),
                   jax.ShapeDtypeStruct((m, D_MODEL // 2), jnp.uint32) if packed
                   else jax.ShapeDtypeStruct((m, D_MODEL), jnp.bfloat16)],
        compiler_params=_params(("arbitrary",)),
        name="mixer_tail",
    )(conv_p, conv_p, conv_p, conv_p, conv_p, pool_gates_p, pool_gates_p, attn,
      *([pool_gates_p] * gate_blocks), x,
      conv_w, pool_scale.reshape(1, POOL_WIDTH), ln_g.reshape(1, D_MODEL), ln_b.reshape(1, D_MODEL),
      w_conv_bf, w_attn_bf, w_pool_bf, pool_mix_bf, w_o_bf)


def _route_tile(y, w_ref, idx_ref, gate_ref, rank_ref, cnt_ref, carry_ref, *, tm):
    @pl.when(pl.program_id(0) == 0)
    def _():
        carry_ref[...] = jnp.zeros_like(carry_ref)

    w = w_ref[...]
    y_hi, w_hi = y.astype(jnp.bfloat16), w.astype(jnp.bfloat16)
    y_lo = (y - y_hi.astype(jnp.float32)).astype(jnp.bfloat16)
    w_lo = (w - w_hi.astype(jnp.float32)).astype(jnp.bfloat16)
    logits = (jnp.dot(y_hi, w_hi, preferred_element_type=jnp.float32)
              + jnp.dot(y_lo, w_hi, preferred_element_type=jnp.float32)
              + jnp.dot(y_hi, w_lo, preferred_element_type=jnp.float32))
    lane = lax.broadcasted_iota(jnp.int32, logits.shape, 1)
    logits = jnp.where(lane < N_EXPERTS, logits, -jnp.inf)
    m1 = jnp.max(logits, axis=-1, keepdims=True)
    i1 = jnp.min(jnp.where(logits == m1, lane, LANES), axis=-1, keepdims=True)
    rest = jnp.where(lane == i1, -jnp.inf, logits)
    m2 = jnp.max(rest, axis=-1, keepdims=True)
    i2 = jnp.min(jnp.where(rest == m2, lane, LANES), axis=-1, keepdims=True)
    e = jnp.exp(m2 - m1)
    den = 1.0 + e
    idx_ref[:, 0:1] = i1
    idx_ref[:, 1:2] = i2
    gate_ref[:, 0:1] = 1.0 / den
    gate_ref[:, 1:2] = e / den

    oh1 = (lane == i1).astype(jnp.float32)
    oh2 = (lane == i2).astype(jnp.float32)
    both = oh1 + oh2
    r = lax.broadcasted_iota(jnp.int32, (tm, tm), 0)
    c = lax.broadcasted_iota(jnp.int32, (tm, tm), 1)
    strict_lower = (r > c).astype(jnp.bfloat16)
    before = jnp.dot(strict_lower, both.astype(jnp.bfloat16), preferred_element_type=jnp.float32)
    before = before + carry_ref[...]
    rank_ref[:, 0:1] = jnp.sum(oh1 * before, axis=-1, keepdims=True).astype(jnp.int32)
    rank_ref[:, 1:2] = jnp.sum(oh2 * before, axis=-1, keepdims=True).astype(jnp.int32)
    carry_ref[...] += jnp.sum(both, axis=0, keepdims=True)
    cnt_ref[...] = carry_ref[...]


def _route_kernel(x_ref, w_ref, idx_ref, gate_ref, rank_ref, cnt_ref, carry_ref, *, tm):
    _route_tile(x_ref[...], w_ref, idx_ref, gate_ref, rank_ref, cnt_ref, carry_ref, tm=tm)


def _route(x, w_router, tm=512):
    m = x.shape[0]
    w_pad = jnp.pad(w_router, ((0, 0), (0, LANES - N_EXPERTS)))
    pair = pl.BlockSpec((tm, TOP_K), lambda i: (i, 0))
    return pl.pallas_call(
        functools.partial(_route_kernel, tm=tm),
        grid=(m // tm,),
        in_specs=[pl.BlockSpec((tm, D_MODEL), lambda i: (i, 0)),
                  pl.BlockSpec((D_MODEL, LANES), lambda i: (0, 0))],
        out_specs=[pair, pair, pair, pl.BlockSpec((1, LANES), lambda i: (0, 0))],
        out_shape=[jax.ShapeDtypeStruct((m, TOP_K), jnp.int32),
                   jax.ShapeDtypeStruct((m, TOP_K), jnp.float32),
                   jax.ShapeDtypeStruct((m, TOP_K), jnp.int32),
                   jax.ShapeDtypeStruct((1, LANES), jnp.float32)],
        scratch_shapes=[pltpu.VMEM((1, LANES), jnp.float32)],
        compiler_params=_params(("arbitrary",)),
        name="router",
    )(x, w_pad)


def _swiglu_kernel(blk_exp_ref, blk_rows_ref, n_used_ref, x_ref, wg_ref, wu_ref, wd_ref, *refs,
                   tm, sub, last_j, resid_ln):
    del blk_exp_ref, n_used_ref
    i, j = pl.program_id(0), pl.program_id(1)
    if resid_ln:
        res_hbm, g_ref, b_ref, o_ref, obf_ref, res_ref, res_sem = refs
        res_copy = pltpu.make_async_copy(res_hbm.at[pl.ds(pl.multiple_of(i * tm, tm), tm)], res_ref, res_sem)
        xbf_ref = x_ref
    else:
        o_ref, xbf_ref = refs
    n_sub = (blk_rows_ref[i] + sub - 1) // sub

    @pl.when(j == 0)
    def _():
        if resid_ln:
            res_copy.start()
        else:
            half = x_ref.shape[1]
            xbf_ref[:, :half], xbf_ref[:, half:] = _unpack_bf16_pairs(x_ref[...])
        o_ref[...] = jnp.zeros_like(o_ref)

    for v in range(1, tm // sub + 1):
        rows = v * sub

        @pl.when(n_sub == v)
        def _(rows=rows):
            x = xbf_ref[0:rows, :]
            gate = jnp.dot(x, wg_ref[0].astype(jnp.bfloat16), preferred_element_type=jnp.float32)
            up = jnp.dot(x, wu_ref[0].astype(jnp.bfloat16), preferred_element_type=jnp.float32)
            h = (gate * jax.nn.sigmoid(gate) * up).astype(jnp.bfloat16)
            o_ref[0:rows, :] += jnp.dot(h, wd_ref[0].astype(jnp.bfloat16),
                                        preferred_element_type=jnp.float32)

    if resid_ln:
        @pl.when(j == last_j)
        def _():
            res_copy.wait()

            def norm_rows(ci, carry):
                rows = pl.ds(pl.multiple_of(ci * sub, sub), sub)
                y = _resid_layer_norm(res_ref[rows, :], o_ref[rows, :], g_ref[...], b_ref[...])
                o_ref[rows, :] = y
                obf_ref[rows, :] = y.astype(jnp.bfloat16)
                return carry

            lax.fori_loop(0, tm // sub, norm_rows, 0)


def _swiglu_blocks(x, w_gate, w_up, w_down, blk_exp, blk_rows, n_used, tm, tf, sub=256, resid_ln=None):
    r = x.shape[0]
    nf = D_FF // tf
    last = nf - 1

    def live(i, nu):
        return jnp.maximum(jnp.minimum(i, nu[0] - 1), 0)

    def jj(i, j, nu):
        return jnp.where(i < nu[0], j, last)

    once = pl.Buffered(1)
    x_spec = pl.BlockSpec((tm, x.shape[1]), lambda i, j, be, br, nu: (live(i, nu), 0), pipeline_mode=once)
    row_out = pl.BlockSpec((tm, D_MODEL), lambda i, j, be, br, nu: (i, 0), pipeline_mode=once)
    vec = pl.BlockSpec((1, D_MODEL), lambda i, j, be, br, nu: (0, 0))
    in_specs = [x_spec,
                pl.BlockSpec((1, D_MODEL, tf), lambda i, j, be, br, nu: (be[live(i, nu)], 0, jj(i, j, nu))),
                pl.BlockSpec((1, D_MODEL, tf), lambda i, j, be, br, nu: (be[live(i, nu)], 0, jj(i, j, nu))),
                pl.BlockSpec((1, tf, D_MODEL), lambda i, j, be, br, nu: (be[live(i, nu)], jj(i, j, nu), 0))]
    args = [x, w_gate, w_up, w_down]
    out_specs = row_out
    out_shape = jax.ShapeDtypeStruct((r, D_MODEL), jnp.float32)
    if resid_ln is not None:
        assert x.dtype == jnp.bfloat16
        res, g, b = resid_ln
        in_specs += [pl.BlockSpec(memory_space=pl.ANY), vec, vec]
        args += [res, g.reshape(1, D_MODEL), b.reshape(1, D_MODEL)]
        out_specs = [row_out, row_out]
        out_shape = [out_shape, jax.ShapeDtypeStruct((r, D_MODEL), jnp.bfloat16)]
        scratch = [pltpu.VMEM((tm, D_MODEL), jnp.float32), pltpu.SemaphoreType.DMA(())]
    else:
        assert x.dtype == jnp.uint32
        scratch = [pltpu.VMEM((tm, D_MODEL), jnp.bfloat16)]
    grid_spec = pltpu.PrefetchScalarGridSpec(
        num_scalar_prefetch=3, grid=(r // tm, nf),
        in_specs=in_specs, out_specs=out_specs, scratch_shapes=scratch)
    return pl.pallas_call(
        functools.partial(_swiglu_kernel, tm=tm, sub=sub, last_j=last, resid_ln=resid_ln is not None),
        grid_spec=grid_spec,
        out_shape=out_shape,
        compiler_params=_params(("arbitrary", "arbitrary")),
        name="swiglu_blocks",
    )(blk_exp, blk_rows, n_used, *args)


ZERO_ROWS = 256


def _dispatch_kernel(zero_slab_ref, dest_ref, x_ref, xs_ref, zero_ref, sem, *, tm):
    @pl.when(pl.program_id(0) == 0)
    def _():
        zero_ref[...] = jnp.zeros_like(zero_ref)
        slabs = [pltpu.make_async_copy(zero_ref, xs_ref.at[pl.ds(z * ZERO_ROWS, ZERO_ROWS)], sem)
                 for z in range(zero_slab_ref.shape[0])]
        for z, slab in enumerate(slabs):
            pl.when(zero_slab_ref[z] != 0)(slab.start)
        for z, slab in enumerate(slabs):
            pl.when(zero_slab_ref[z] != 0)(slab.wait)

    def issue(t, carry):
        for k in range(TOP_K):
            pltpu.make_async_copy(x_ref.at[pl.ds(t, 1)],
                                  xs_ref.at[pl.ds(dest_ref[TOP_K * t + k], 1)], sem).start()
        return carry

    lax.fori_loop(0, tm, issue, 0, unroll=8)
    for k in range(TOP_K):
        pltpu.make_async_copy(x_ref, xs_ref.at[pl.ds(0, tm)], sem).wait()


def _dispatch(x, dest_flat, zero_slab, rows, tm=512):
    m, width = x.shape
    grid_spec = pltpu.PrefetchScalarGridSpec(
        num_scalar_prefetch=1, grid=(m // tm,),
        in_specs=[pl.BlockSpec((TOP_K * tm,), lambda i, zs: (i,), memory_space=pltpu.SMEM),
                  pl.BlockSpec((tm, width), lambda i, zs: (i, 0))],
        out_specs=pl.BlockSpec(memory_space=pl.ANY),
        scratch_shapes=[pltpu.VMEM((ZERO_ROWS, width), x.dtype), pltpu.SemaphoreType.DMA(())])
    return pl.pallas_call(
        functools.partial(_dispatch_kernel, tm=tm),
        grid_spec=grid_spec,
        out_shape=jax.ShapeDtypeStruct((rows, width), x.dtype),
        compiler_params=_params(("arbitrary",)),
        name="dispatch",
    )(zero_slab, dest_flat, x)


def _combine_ln_kernel(dest_ref, dest_next_ref, x_ref, gate_ref, g_ref, b_ref, y_ref, o_ref,
                       ybuf_ref, sems, *, tm):
    i, n = pl.program_id(0), pl.num_programs(0)
    slot = lax.rem(i, 2)

    def gather(d_ref, s):
        def issue(t, carry):
            for k in range(TOP_K):
                pltpu.make_async_copy(y_ref.at[pl.ds(d_ref[TOP_K * t + k], 1)],
                                      ybuf_ref.at[s, k, pl.ds(t, 1)], sems.at[s]).start()
            return carry

        lax.fori_loop(0, tm, issue, 0, unroll=8)

    @pl.when(i == 0)
    def _():
        gather(dest_ref, 0)

    @pl.when(i + 1 < n)
    def _():
        gather(dest_next_ref, 1 - slot)

    for k in range(TOP_K):
        pltpu.make_async_copy(y_ref.at[pl.ds(0, tm)], ybuf_ref.at[slot, k], sems.at[slot]).wait()
    gate = gate_ref[...]
    f = ybuf_ref[slot, 0] * gate[:, 0:1] + ybuf_ref[slot, 1] * gate[:, 1:2]
    o_ref[...] = _resid_layer_norm(x_ref[...], f, g_ref[...], b_ref[...])


def _combine_ln(x, y, dest_flat, gates, g, b, tm=512):
    m = x.shape[0]
    n_tiles = m // tm
    row = pl.BlockSpec((tm, D_MODEL), lambda i: (i, 0))
    vec = pl.BlockSpec((1, D_MODEL), lambda i: (0, 0))
    return pl.pallas_call(
        functools.partial(_combine_ln_kernel, tm=tm),
        grid=(n_tiles,),
        in_specs=[pl.BlockSpec((TOP_K * tm,), lambda i: (i,), memory_space=pltpu.SMEM),
                  pl.BlockSpec((TOP_K * tm,), lambda i: (jnp.minimum(i + 1, n_tiles - 1),),
                               memory_space=pltpu.SMEM),
                  row, pl.BlockSpec((tm, TOP_K), lambda i: (i, 0)), vec, vec,
                  pl.BlockSpec(memory_space=pl.ANY)],
        out_specs=row,
        out_shape=jax.ShapeDtypeStruct((m, D_MODEL), jnp.float32),
        scratch_shapes=[pltpu.VMEM((2, TOP_K, tm, D_MODEL), jnp.float32),
                        pltpu.SemaphoreType.DMA((2,))],
        compiler_params=_params(("arbitrary",)),
        name="combine_ln",
    )(dest_flat, dest_flat, x, gates, g.reshape(1, D_MODEL), b.reshape(1, D_MODEL), y)


def _rope_tables(seq):
    inv_freq = ROPE_THETA ** (-jnp.arange(0, ROPE_DIM, 2, dtype=jnp.float32) / ROPE_DIM)
    ang = jnp.arange(seq, dtype=jnp.float32)[:, None] * inv_freq[None, :]
    cos, sin = jnp.cos(ang), jnp.sin(ang)
    pad = HEAD_DIM - ROPE_DIM
    cos_full = jnp.concatenate([cos, cos, jnp.ones((seq, pad), jnp.float32)], axis=1)
    sin_full = jnp.concatenate([-sin, sin, jnp.zeros((seq, pad), jnp.float32)], axis=1)
    return cos_full, sin_full


def _mixer(x, x_bf, batch, w_in, conv_w, w_br_conv, w_br_attn, pool_mix, pool_scale, w_br_pool, w_o,
           ln_g, ln_b, rope_cos, rope_sin, packed=False):
    conv_cols = CONV_KERNEL * CONV_WIDTH
    pool_col = conv_cols + QKV_WIDTH
    conv_p = _project(x_bf, w_in, 0, conv_cols, 1024, jnp.bfloat16)
    qkv = _project(x_bf, w_in, conv_cols, QKV_WIDTH, QKV_WIDTH // 3, jnp.bfloat16,
                   rope=(rope_cos, rope_sin), rope_tiles=2)
    pool_gates_p = _project(x_bf, w_in, pool_col, POOL_WIDTH + N_GROUPS * D_MODEL, 1024, jnp.bfloat16)
    attn = _attention(qkv.reshape(batch, SEQ, QKV_WIDTH))
    attn = attn.reshape(batch * SEQ, GROUP_WIDTH)
    bf = lambda w: w.astype(jnp.bfloat16)
    return _mixer_tail(conv_p, pool_gates_p, attn, x, conv_w, pool_scale, ln_g, ln_b,
                       bf(w_br_conv), bf(w_br_attn), bf(w_br_pool), bf(pool_mix), bf(w_o), packed=packed)


def _dense_ffn(x, x_bf, w_gate, w_up, w_down, ln_g, ln_b, tm=1024, tf=512):
    n_blk = x.shape[0] // tm
    blk_exp = jnp.zeros((n_blk,), jnp.int32)
    blk_rows = jnp.full((n_blk,), tm, jnp.int32)
    n_used = jnp.full((1,), n_blk, jnp.int32)
    return _swiglu_blocks(x_bf, w_gate[None], w_up[None], w_down[None], blk_exp, blk_rows, n_used, tm, tf,
                          resid_ln=(x, ln_g, ln_b))


def _moe_ffn(x, x_packed, w_router, w_gate, w_up, w_down, ln_g, ln_b, fill=1024, tf=512, sub=256):
    m = x.shape[0]
    tm = fill + sub
    idx, gates, rank, cnt = _route(x, w_router)
    counts = cnt[0, :N_EXPERTS].astype(jnp.int32)
    n_blocks = jnp.where(counts > 0, jnp.maximum((counts - sub + fill - 1) // fill, 1), 0)
    blk_end = jnp.cumsum(n_blocks)
    blk_first = blk_end - n_blocks
    in_expert = jnp.minimum(rank // fill, n_blocks[idx] - 1)
    dest_flat = ((blk_first[idx] + in_expert) * tm + rank - in_expert * fill).reshape(-1)
    n_blk = m * TOP_K // fill + N_EXPERTS
    blk = jnp.arange(n_blk, dtype=jnp.int32)
    blk_exp = jnp.minimum(jnp.sum(blk[:, None] >= blk_end[None, :], axis=1), N_EXPERTS - 1).astype(jnp.int32)
    is_last = blk - blk_first[blk_exp] == n_blocks[blk_exp] - 1
    blk_rows = jnp.where(is_last, counts[blk_exp] - fill * (n_blocks[blk_exp] - 1), fill)
    n_used = blk_end[-1:].astype(jnp.int32)
    blk_rows = jnp.where(blk < n_used[0], blk_rows, 0).astype(jnp.int32)
    slabs_per_blk = tm // ZERO_ROWS
    slab_end = (jnp.arange(n_blk * slabs_per_blk, dtype=jnp.int32) % slabs_per_blk + 1) * ZERO_ROWS
    zero_slab = (slab_end > jnp.repeat(blk_rows, slabs_per_blk)).astype(jnp.int32)
    xs = _dispatch(x_packed, dest_flat, zero_slab, n_blk * tm)
    ys = _swiglu_blocks(xs, w_gate, w_up, w_down, blk_exp, blk_rows, n_used, tm, tf, sub=sub)
    return _combine_ln(x, ys, dest_flat, gates, ln_g, ln_b)


def kernel(x, l0_w_in, l0_conv_w, l0_w_br_conv, l0_w_br_attn, l0_pool_mix, l0_pool_scale, l0_w_br_pool, l0_w_o, l0_ln1_g, l0_ln1_b, l0_ffn_gate, l0_ffn_up, l0_ffn_down, l0_ln2_g, l0_ln2_b, l1_w_in, l1_conv_w, l1_w_br_conv, l1_w_br_attn, l1_pool_mix, l1_pool_scale, l1_w_br_pool, l1_w_o, l1_ln1_g, l1_ln1_b, l1_router, l1_exp_gate, l1_exp_up, l1_exp_down, l1_ln2_g, l1_ln2_b):
    batch, seq, d = x.shape
    assert seq == SEQ and d == D_MODEL
    rope_cos, rope_sin = _rope_tables(seq)
    x0 = x.reshape(batch * seq, d)
    x1, x1_bf = _mixer(x0, x0.astype(jnp.bfloat16), batch, l0_w_in, l0_conv_w, l0_w_br_conv, l0_w_br_attn,
                       l0_pool_mix, l0_pool_scale, l0_w_br_pool, l0_w_o, l0_ln1_g, l0_ln1_b, rope_cos, rope_sin)
    x2, x2_bf = _dense_ffn(x1, x1_bf, l0_ffn_gate, l0_ffn_up, l0_ffn_down, l0_ln2_g, l0_ln2_b)
    x3, x3_packed = _mixer(x2, x2_bf, batch, l1_w_in, l1_conv_w, l1_w_br_conv, l1_w_br_attn,
                           l1_pool_mix, l1_pool_scale, l1_w_br_pool, l1_w_o, l1_ln1_g, l1_ln1_b,
                           rope_cos, rope_sin, packed=True)
    x4 = _moe_ffn(x3, x3_packed, l1_router, l1_exp_gate, l1_exp_up, l1_exp_down, l1_ln2_g, l1_ln2_b)
    return x4.reshape(batch, seq, d)
```

```python
import functools

import jax
import jax.numpy as jnp
from jax import lax
from jax.experimental import pallas as pl
from jax.experimental.pallas import tpu as pltpu

D_MODEL = 2048
SEQ = 4096
CONV_WIDTH = D_MODEL // 2
CONV_KERNEL = 3
HEAD_DIM = 128
HEADS_PER_GROUP = 4
GROUP_WIDTH = HEADS_PER_GROUP * HEAD_DIM
ATTN_PATTERNS = ((128, 1), (512, 4), (2048, 16))
N_GROUPS = len(ATTN_PATTERNS)
ATTN_HEADS = N_GROUPS * HEADS_PER_GROUP
QKV_WIDTH = 3 * N_GROUPS * GROUP_WIDTH
ATTN_BLOCK = 128
ROPE_THETA = 500000.0
ROPE_DIM = HEAD_DIM // 4
POOL_WINDOWS = (2, 4, 8, 16)
POOL_WIDTH = D_MODEL // 2
POOL_GROUP_DIM = POOL_WIDTH // len(POOL_WINDOWS)
POOL_HALO = 16
CONV_HALO = 16
D_FF = 7 * D_MODEL // 2
N_EXPERTS = 8
TOP_K = 2
LN_EPS = 1e-5
DEPTH = 2
DEEPNORM_ALPHA = (2.0 * DEPTH) ** 0.25
NEG_INF = -1e30
LANES = 128

VMEM_LIMIT = 56 * 1024 * 1024


def _params(sem, vmem=VMEM_LIMIT):
    return pltpu.CompilerParams(dimension_semantics=sem, vmem_limit_bytes=vmem)


W_SLAB = 512
PROJ_CHUNK = 1024


def _rope(t, cos, sin):
    half = ROPE_DIM // 2
    lane = lax.broadcasted_iota(jnp.int32, t.shape, 1)
    partner = jnp.where(lane < half, pltpu.roll(t, HEAD_DIM - half, 1), pltpu.roll(t, half, 1))
    return t * cos + partner * sin


def _proj_kernel(*refs, n_slabs, rope_tiles):
    x_ref, w_refs = refs[0], refs[1:1 + n_slabs]
    o_ref, wbf_ref = refs[-2], refs[-1]

    @pl.when(pl.program_id(1) == 0)
    def _():
        for t, w_ref in enumerate(w_refs):
            wbf_ref[:, t * W_SLAB:(t + 1) * W_SLAB] = w_ref[...].astype(jnp.bfloat16)

    def plain():
        tn = o_ref.shape[1]
        for c0 in range(0, tn, PROJ_CHUNK):
            cols = slice(c0, min(c0 + PROJ_CHUNK, tn))
            o_ref[:, cols] = jnp.dot(x_ref[...], wbf_ref[:, cols],
                                     preferred_element_type=jnp.float32).astype(o_ref.dtype)

    if rope_tiles == 0:
        plain()
        return
    cos_ref, sin_ref = refs[1 + n_slabs], refs[2 + n_slabs]
    is_rope_tile = pl.program_id(0) < rope_tiles

    @pl.when(is_rope_tile)
    def _():
        cos, sin = cos_ref[...], sin_ref[...]
        for t in range(n_slabs):
            acc = jnp.dot(x_ref[...], wbf_ref[:, t * W_SLAB:(t + 1) * W_SLAB],
                          preferred_element_type=jnp.float32)
            for h in range(W_SLAB // HEAD_DIM):
                cols = slice(h * HEAD_DIM, (h + 1) * HEAD_DIM)
                out_cols = slice(t * W_SLAB + h * HEAD_DIM, t * W_SLAB + (h + 1) * HEAD_DIM)
                o_ref[:, out_cols] = _rope(acc[:, cols], cos, sin).astype(o_ref.dtype)

    pl.when(jnp.logical_not(is_rope_tile))(plain)


def _project(x_bf, w, col0, width, tn, out_dtype, tm=1024, rope=None, rope_tiles=0):
    m, k = x_bf.shape
    assert col0 % W_SLAB == 0 and tn % W_SLAB == 0 and width % tn == 0 and m % tm == 0
    n_slabs = tn // W_SLAB
    slab0 = col0 // W_SLAB
    w_specs = [pl.BlockSpec((k, W_SLAB), lambda j, i, t=t: (0, slab0 + j * n_slabs + t))
               for t in range(n_slabs)]
    rope_specs, rope_args = [], []
    if rope_tiles:
        tiles_per_seq = SEQ // tm
        rope_specs = [pl.BlockSpec((tm, LANES), lambda j, i: (i % tiles_per_seq, 0))] * 2
        rope_args = list(rope)
    return pl.pallas_call(
        functools.partial(_proj_kernel, n_slabs=n_slabs, rope_tiles=rope_tiles),
        grid=(width // tn, m // tm),
        in_specs=[pl.BlockSpec((tm, k), lambda j, i: (i, 0))] + w_specs + rope_specs,
        out_specs=pl.BlockSpec((tm, tn), lambda j, i: (i, j)),
        out_shape=jax.ShapeDtypeStruct((m, width), out_dtype),
        scratch_shapes=[pltpu.VMEM((k, tn), jnp.bfloat16)],
        compiler_params=_params(("arbitrary", "arbitrary")),
        name="in_proj",
    )(x_bf, *([w] * n_slabs), *rope_args)


def _strided(start, size, stride):
    return pl.ds(start, size) if stride == 1 else pl.ds(start, size, stride=stride)


def _attn_kernel(q0_ref, k0_ref, v0_ref, q1_ref, k1_ref, v1_ref, q2_ref, k2_ref, v2_ref,
                 o_ref, q_s, k_s, v_s, out_s, lse_s, *, chunk, chains):
    groups = ((q0_ref, k0_ref, v0_ref), (q1_ref, k1_ref, v1_ref), (q2_ref, k2_ref, v2_ref))
    n_keys = 2 * ATTN_BLOCK
    a = lax.broadcasted_iota(jnp.int32, (ATTN_BLOCK, n_keys), 0)
    kk = lax.broadcasted_iota(jnp.int32, (ATTN_BLOCK, n_keys), 1)
    scale = HEAD_DIM ** -0.5

    for g, (q_ref, k_ref, v_ref) in enumerate(groups):
        d = ATTN_PATTERNS[g][1]
        if d > 1:
            def stage(ci, carry, q_ref=q_ref, k_ref=k_ref, v_ref=v_ref):
                rows = pl.ds(pl.multiple_of(ci * chunk, chunk), chunk)
                q_s[rows, :] = q_ref[0, rows, :].astype(jnp.float32)
                k_s[rows, :] = k_ref[0, rows, :].astype(jnp.float32)
                v_s[rows, :] = v_ref[0, rows, :].astype(jnp.float32)
                return carry

            lax.fori_loop(0, SEQ // chunk, stage, 0)
            q_src, k_src, v_src = q_s, k_s, v_s
        else:
            q_src, k_src, v_src = q_ref.at[0], k_ref.at[0], v_ref.at[0]

        blocks_per_residue = SEQ // d // ATTN_BLOCK
        unroll = min(chains, blocks_per_residue)
        residues_per_body = chains // unroll
        for c0 in range(0, d, residues_per_body):
            def block(blk, carry, c0=c0, d=d, g=g, residues_per_body=residues_per_body,
                      q_src=q_src, k_src=k_src, v_src=v_src):
                base = pl.multiple_of(blk * (ATTN_BLOCK * d), ATTN_BLOCK)
                key_base = pl.multiple_of(jnp.maximum(base - ATTN_BLOCK * d, 0), ATTN_BLOCK)
                self_key = jnp.where(blk == 0, 0, ATTN_BLOCK)
                valid = (kk >= a + self_key - ATTN_BLOCK) & (kk <= a + self_key)
                for c in range(c0, c0 + residues_per_body):
                    qb = q_src[_strided(base + c, ATTN_BLOCK, d), :].astype(jnp.bfloat16)
                    kb = k_src[_strided(key_base + c, n_keys, d), :].astype(jnp.bfloat16)
                    vb = v_src[_strided(key_base + c, n_keys, d), :].astype(jnp.bfloat16)
                    s = lax.dot_general(qb, kb, (((1,), (1,)), ((), ())),
                                        preferred_element_type=jnp.float32) * scale
                    s = jnp.where(valid, s, NEG_INF)
                    m = jnp.max(s, axis=-1, keepdims=True)
                    p = jnp.exp(s - m)
                    l = jnp.sum(p, axis=-1, keepdims=True)
                    o = jnp.dot(p.astype(jnp.bfloat16), vb, preferred_element_type=jnp.float32)
                    rows = _strided(base + c, ATTN_BLOCK, d)
                    out_s[g, rows, :] = o / l
                    lse_s[g, rows, :] = jnp.broadcast_to(m + jnp.log(l), (ATTN_BLOCK, HEAD_DIM))
                return carry

            lax.fori_loop(0, blocks_per_residue, block, 0, unroll=unroll)

    def combine(ci, carry):
        rows = pl.ds(pl.multiple_of(ci * chunk, chunk), chunk)
        l0, l1, l2 = lse_s[0, rows, :], lse_s[1, rows, :], lse_s[2, rows, :]
        mx = jnp.maximum(jnp.maximum(l0, l1), l2)
        e0, e1, e2 = jnp.exp(l0 - mx), jnp.exp(l1 - mx), jnp.exp(l2 - mx)
        num = e0 * out_s[0, rows, :] + e1 * out_s[1, rows, :] + e2 * out_s[2, rows, :]
        o_ref[0, rows, :] = (num / (e0 + e1 + e2)).astype(o_ref.dtype)
        return carry

    lax.fori_loop(0, SEQ // chunk, combine, 0)


def _attention(qkv, chunk=256, chains=8):
    b, s, _ = qkv.shape
    assert s == SEQ

    def head_block(part, group):
        col0 = part * ATTN_HEADS + group * HEADS_PER_GROUP
        return pl.BlockSpec((1, s, HEAD_DIM), lambda bi, h: (bi, 0, col0 + h))

    in_specs = [head_block(part, group) for group in range(N_GROUPS) for part in range(3)]
    return pl.pallas_call(
        functools.partial(_attn_kernel, chunk=chunk, chains=chains),
        grid=(b, HEADS_PER_GROUP),
        in_specs=in_specs,
        out_specs=pl.BlockSpec((1, s, HEAD_DIM), lambda bi, h: (bi, 0, h)),
        out_shape=jax.ShapeDtypeStruct((b, s, GROUP_WIDTH), jnp.bfloat16),
        scratch_shapes=[pltpu.VMEM((s, HEAD_DIM), jnp.float32),
                        pltpu.VMEM((s, HEAD_DIM), jnp.float32),
                        pltpu.VMEM((s, HEAD_DIM), jnp.float32),
                        pltpu.VMEM((N_GROUPS, s, HEAD_DIM), jnp.float32),
                        pltpu.VMEM((N_GROUPS, s, HEAD_DIM), jnp.float32)],
        compiler_params=_params(("arbitrary", "arbitrary")),
        name="dilated_attn",
    )(*([qkv] * 9))


def _resid_layer_norm(x, f, g, b):
    z = DEEPNORM_ALPHA * x + f
    mu = jnp.mean(z, axis=-1, keepdims=True)
    zc = z - mu
    var = jnp.mean(zc * zc, axis=-1, keepdims=True)
    return zc * lax.rsqrt(var + LN_EPS) * g + b


def _pack_bf16_pairs(y):
    c = y.shape[1] // 2
    bits = pltpu.bitcast(y.astype(jnp.bfloat16).astype(jnp.float32), jnp.uint32)
    return bits[:, :c] | (bits[:, c:] >> 16)


def _unpack_bf16_pairs(w):
    hi = pltpu.bitcast(w & jnp.uint32(0xFFFF0000), jnp.float32).astype(jnp.bfloat16)
    lo = pltpu.bitcast(w << 16, jnp.float32).astype(jnp.bfloat16)
    return hi, lo


def _mixer_tail_kernel(cb_ref, cc_ref, ch_ref, cch_ref, chh_ref, pu_ref, puh_ref, attn_ref,
                       g0a_ref, g0b_ref, g1a_ref, g1b_ref, g2a_ref, g2b_ref, x_ref,
                       convw_ref, pscale_ref, lng_ref, lnb_ref,
                       wconv_ref, wattn_ref, wpool_ref, pmix_ref, wo_ref, o_ref, obf_ref,
                       *, tm, tiles_per_seq, packed):
    seq_tile = pl.program_id(0) % tiles_per_seq
    not_first = (seq_tile != 0).astype(jnp.float32)
    f32 = lambda ref: ref[...].astype(jnp.float32)

    u = f32(cc_ref) * f32(ch_ref)
    uh = f32(cch_ref) * f32(chh_ref) * not_first
    ext = jnp.concatenate([uh, u], axis=0)
    u1 = pltpu.roll(ext, 1, 0)[CONV_HALO:]
    u2 = pltpu.roll(ext, 2, 0)[CONV_HALO:]
    cw = convw_ref[...]
    conv = u2 * cw[0:1] + u1 * cw[1:2]
    conv = conv + u * cw[2:3]
    ya_in = (f32(cb_ref) * conv).astype(jnp.bfloat16)

    pu = f32(pu_ref)
    pext = jnp.concatenate([f32(puh_ref) * not_first, pu], axis=0)
    pos = seq_tile * tm + lax.broadcasted_iota(jnp.int32, (tm, 1), 0) + 1
    pscale = pscale_ref[...]
    mixed = []
    for gi, w in enumerate(POOL_WINDOWS):
        cols = slice(gi * POOL_GROUP_DIM, (gi + 1) * POOL_GROUP_DIM)
        run = pext[:, cols]
        step = 1
        while step < w:
            run = run + pltpu.roll(run, step, 0)
            step *= 2
        inv_cnt = 1.0 / jnp.minimum(pos, w).astype(jnp.float32)
        pooled = run[POOL_HALO:] * inv_cnt - pu[:, cols]
        mix = jnp.dot(pooled.astype(jnp.bfloat16), pmix_ref[gi], preferred_element_type=jnp.float32)
        mixed.append((mix * pscale[:, cols]).astype(jnp.bfloat16))
    yc_in = jnp.concatenate(mixed, axis=1)

    halves = []
    for hc, (ga, gb, gc) in enumerate(((g0a_ref, g1a_ref, g2a_ref), (g0b_ref, g1b_ref, g2b_ref))):
        cols = slice(hc * CONV_WIDTH, (hc + 1) * CONV_WIDTH)
        sig = lambda ref: jax.nn.sigmoid(f32(ref))
        part = sig(ga) * jnp.dot(ya_in, wconv_ref[:, cols], preferred_element_type=jnp.float32)
        part = part + sig(gb) * jnp.dot(attn_ref[...], wattn_ref[:, cols], preferred_element_type=jnp.float32)
        part = part + sig(gc) * jnp.dot(yc_in, wpool_ref[:, cols], preferred_element_type=jnp.float32)
        halves.append(part.astype(jnp.bfloat16))
    merged = jnp.concatenate(halves, axis=1)

    f = jnp.dot(merged, wo_ref[...], preferred_element_type=jnp.float32)
    y = _resid_layer_norm(x_ref[...], f, lng_ref[...], lnb_ref[...])
    o_ref[...] = y
    obf_ref[...] = _pack_bf16_pairs(y) if packed else y.astype(jnp.bfloat16)


def _mixer_tail(conv_p, pool_gates_p, attn, x, conv_w, pool_scale, ln_g, ln_b,
                w_conv_bf, w_attn_bf, w_pool_bf, pool_mix_bf, w_o_bf, packed=False, tm=256):
    m = conv_p.shape[0]
    gate_blocks = N_GROUPS * D_MODEL // CONV_WIDTH
    tiles_per_seq = SEQ // tm
    row = lambda width, col: pl.BlockSpec((tm, width), lambda i: (i, col))

    def halo(rows, col):
        per_tile = tm // rows
        return pl.BlockSpec((rows, CONV_WIDTH), lambda i: (jnp.maximum(i * per_tile - 1, 0), col))

    full = lambda shape: pl.BlockSpec(shape, lambda i: (0,) * len(shape), pipeline_mode=pl.Buffered(1))
    in_specs = [row(CONV_WIDTH, 0), row(CONV_WIDTH, 1), row(CONV_WIDTH, 2),
                halo(CONV_HALO, 1), halo(CONV_HALO, 2),
                row(POOL_WIDTH, 0), halo(POOL_HALO, 0),
                row(GROUP_WIDTH, 0)]
    in_specs += [row(CONV_WIDTH, 1 + gb) for gb in range(gate_blocks)]
    in_specs += [row(D_MODEL, 0),
                 full((CONV_KERNEL, CONV_WIDTH)), full((1, POOL_WIDTH)), full((1, D_MODEL)), full((1, D_MODEL)),
                 full((CONV_WIDTH, D_MODEL)), full((GROUP_WIDTH, D_MODEL)), full((POOL_WIDTH, D_MODEL)),
                 full((len(POOL_WINDOWS), POOL_GROUP_DIM, POOL_GROUP_DIM)), full((D_MODEL, D_MODEL))]
    return pl.pallas_call(
        functools.partial(_mixer_tail_kernel, tm=tm, tiles_per_seq=tiles_per_seq, packed=packed),
        grid=(m // tm,),
        in_specs=in_specs,
        out_specs=[row(D_MODEL, 0), row(D_MODEL // 2 if packed else D_MODEL, 0)],
        out_shape=[jax.ShapeDtypeStruct((m, D_MODEL), jnp.float32),
                   jax.ShapeDtypeStruct((m, D_MODEL // 2), jnp.uint32) if packed
                   else jax.ShapeDtypeStruct((m, D_MODEL), jnp.bfloat16)],
        compiler_params=_params(("arbitrary",)),
        name="mixer_tail",
    )(conv_p, conv_p, conv_p, conv_p, conv_p, pool_gates_p, pool_gates_p, attn,
      *([pool_gates_p] * gate_blocks), x,
      conv_w, pool_scale.reshape(1, POOL_WIDTH), ln_g.reshape(1, D_MODEL), ln_b.reshape(1, D_MODEL),
      w_conv_bf, w_attn_bf, w_pool_bf, pool_mix_bf, w_o_bf)


def _route_tile(y, w_ref, idx_ref, gate_ref, rank_ref, cnt_ref, carry_ref, *, tm):
    @pl.when(pl.program_id(0) == 0)
    def _():
        carry_ref[...] = jnp.zeros_like(carry_ref)

    w = w_ref[...]
    y_hi, w_hi = y.astype(jnp.bfloat16), w.astype(jnp.bfloat16)
    y_lo = (y - y_hi.astype(jnp.float32)).astype(jnp.bfloat16)
    w_lo = (w - w_hi.astype(jnp.float32)).astype(jnp.bfloat16)
    logits = (jnp.dot(y_hi, w_hi, preferred_element_type=jnp.float32)
              + jnp.dot(y_lo, w_hi, preferred_element_type=jnp.float32)
              + jnp.dot(y_hi, w_lo, preferred_element_type=jnp.float32))
    lane = lax.broadcasted_iota(jnp.int32, logits.shape, 1)
    logits = jnp.where(lane < N_EXPERTS, logits, -jnp.inf)
    m1 = jnp.max(logits, axis=-1, keepdims=True)
    i1 = jnp.min(jnp.where(logits == m1, lane, LANES), axis=-1, keepdims=True)
    rest = jnp.where(lane == i1, -jnp.inf, logits)
    m2 = jnp.max(rest, axis=-1, keepdims=True)
    i2 = jnp.min(jnp.where(rest == m2, lane, LANES), axis=-1, keepdims=True)
    e = jnp.exp(m2 - m1)
    den = 1.0 + e
    idx_ref[:, 0:1] = i1
    idx_ref[:, 1:2] = i2
    gate_ref[:, 0:1] = 1.0 / den
    gate_ref[:, 1:2] = e / den

    oh1 = (lane == i1).astype(jnp.float32)
    oh2 = (lane == i2).astype(jnp.float32)
    both = oh1 + oh2
    r = lax.broadcasted_iota(jnp.int32, (tm, tm), 0)
    c = lax.broadcasted_iota(jnp.int32, (tm, tm), 1)
    strict_lower = (r > c).astype(jnp.bfloat16)
    before = jnp.dot(strict_lower, both.astype(jnp.bfloat16), preferred_element_type=jnp.float32)
    before = before + carry_ref[...]
    rank_ref[:, 0:1] = jnp.sum(oh1 * before, axis=-1, keepdims=True).astype(jnp.int32)
    rank_ref[:, 1:2] = jnp.sum(oh2 * before, axis=-1, keepdims=True).astype(jnp.int32)
    carry_ref[...] += jnp.sum(both, axis=0, keepdims=True)
    cnt_ref[...] = carry_ref[...]


def _route_kernel(x_ref, w_ref, idx_ref, gate_ref, rank_ref, cnt_ref, carry_ref, *, tm):
    _route_tile(x_ref[...], w_ref, idx_ref, gate_ref, rank_ref, cnt_ref, carry_ref, tm=tm)


def _route(x, w_router, tm=512):
    m = x.shape[0]
    w_pad = jnp.pad(w_router, ((0, 0), (0, LANES - N_EXPERTS)))
    pair = pl.BlockSpec((tm, TOP_K), lambda i: (i, 0))
    return pl.pallas_call(
        functools.partial(_route_kernel, tm=tm),
        grid=(m // tm,),
        in_specs=[pl.BlockSpec((tm, D_MODEL), lambda i: (i, 0)),
                  pl.BlockSpec((D_MODEL, LANES), lambda i: (0, 0))],
        out_specs=[pair, pair, pair, pl.BlockSpec((1, LANES), lambda i: (0, 0))],
        out_shape=[jax.ShapeDtypeStruct((m, TOP_K), jnp.int32),
                   jax.ShapeDtypeStruct((m, TOP_K), jnp.float32),
                   jax.ShapeDtypeStruct((m, TOP_K), jnp.int32),
                   jax.ShapeDtypeStruct((1, LANES), jnp.float32)],
        scratch_shapes=[pltpu.VMEM((1, LANES), jnp.float32)],
        compiler_params=_params(("arbitrary",)),
        name="router",
    )(x, w_pad)


def _swiglu_kernel(blk_exp_ref, blk_rows_ref, n_used_ref, x_ref, wg_ref, wu_ref, wd_ref, *refs,
                   tm, sub, last_j, resid_ln):
    del blk_exp_ref, n_used_ref
    i, j = pl.program_id(0), pl.program_id(1)
    if resid_ln:
        res_hbm, g_ref, b_ref, o_ref, obf_ref, res_ref, res_sem = refs
        res_copy = pltpu.make_async_copy(res_hbm.at[pl.ds(pl.multiple_of(i * tm, tm), tm)], res_ref, res_sem)
        xbf_ref = x_ref
    else:
        o_ref, xbf_ref = refs
    n_sub = (blk_rows_ref[i] + sub - 1) // sub

    @pl.when(j == 0)
    def _():
        if resid_ln:
            res_copy.start()
        else:
            half = x_ref.shape[1]
            xbf_ref[:, :half], xbf_ref[:, half:] = _unpack_bf16_pairs(x_ref[...])
        o_ref[...] = jnp.zeros_like(o_ref)

    for v in range(1, tm // sub + 1):
        rows = v * sub

        @pl.when(n_sub == v)
        def _(rows=rows):
            x = xbf_ref[0:rows, :]
            gate = jnp.dot(x, wg_ref[0].astype(jnp.bfloat16), preferred_element_type=jnp.float32)
            up = jnp.dot(x, wu_ref[0].astype(jnp.bfloat16), preferred_element_type=jnp.float32)
            h = (gate * jax.nn.sigmoid(gate) * up).astype(jnp.bfloat16)
            o_ref[0:rows, :] += jnp.dot(h, wd_ref[0].astype(jnp.bfloat16),
                                        preferred_element_type=jnp.float32)

    if resid_ln:
        @pl.when(j == last_j)
        def _():
            res_copy.wait()

            def norm_rows(ci, carry):
                rows = pl.ds(pl.multiple_of(ci * sub, sub), sub)
                y = _resid_layer_norm(res_ref[rows, :], o_ref[rows, :], g_ref[...], b_ref[...])
                o_ref[rows, :] = y
                obf_ref[rows, :] = y.astype(jnp.bfloat16)
                return carry

            lax.fori_loop(0, tm // sub, norm_rows, 0)


def _swiglu_blocks(x, w_gate, w_up, w_down, blk_exp, blk_rows, n_used, tm, tf, sub=256, resid_ln=None):
    r = x.shape[0]
    nf = D_FF // tf
    last = nf - 1

    def live(i, nu):
        return jnp.maximum(jnp.minimum(i, nu[0] - 1), 0)

    def jj(i, j, nu):
        return jnp.where(i < nu[0], j, last)

    once = pl.Buffered(1)
    x_spec = pl.BlockSpec((tm, x.shape[1]), lambda i, j, be, br, nu: (live(i, nu), 0),
                          pipeline_mode=once if resid_ln is not None else None)
    row_out = pl.BlockSpec((tm, D_MODEL), lambda i, j, be, br, nu: (i, 0), pipeline_mode=once)
    vec = pl.BlockSpec((1, D_MODEL), lambda i, j, be, br, nu: (0, 0))
    in_specs = [x_spec,
                pl.BlockSpec((1, D_MODEL, tf), lambda i, j, be, br, nu: (be[live(i, nu)], 0, jj(i, j, nu))),
                pl.BlockSpec((1, D_MODEL, tf), lambda i, j, be, br, nu: (be[live(i, nu)], 0, jj(i, j, nu))),
                pl.BlockSpec((1, tf, D_MODEL), lambda i, j, be, br, nu: (be[live(i, nu)], jj(i, j, nu), 0))]
    args = [x, w_gate, w_up, w_down]
    out_specs = row_out
    out_shape = jax.ShapeDtypeStruct((r, D_MODEL), jnp.float32)
    if resid_ln is not None:
        assert x.dtype == jnp.bfloat16
        res, g, b = resid_ln
        in_specs += [pl.BlockSpec(memory_space=pl.ANY), vec, vec]
        args += [res, g.reshape(1, D_MODEL), b.reshape(1, D_MODEL)]
        out_specs = [row_out, row_out]
        out_shape = [out_shape, jax.ShapeDtypeStruct((r, D_MODEL), jnp.bfloat16)]
        scratch = [pltpu.VMEM((tm, D_MODEL), jnp.float32), pltpu.SemaphoreType.DMA(())]
    else:
        assert x.dtype == jnp.uint32
        scratch = [pltpu.VMEM((tm, D_MODEL), jnp.bfloat16)]
    grid_spec = pltpu.PrefetchScalarGridSpec(
        num_scalar_prefetch=3, grid=(r // tm, nf),
        in_specs=in_specs, out_specs=out_specs, scratch_shapes=scratch)
    return pl.pallas_call(
        functools.partial(_swiglu_kernel, tm=tm, sub=sub, last_j=last, resid_ln=resid_ln is not None),
        grid_spec=grid_spec,
        out_shape=out_shape,
        compiler_params=_params(("arbitrary", "arbitrary")),
        name="swiglu_blocks",
    )(blk_exp, blk_rows, n_used, *args)


ZERO_ROWS = 256


def _dispatch_kernel(zero_slab_ref, dest_ref, x_ref, xs_ref, zero_ref, sem, *, tm):
    @pl.when(pl.program_id(0) == 0)
    def _():
        zero_ref[...] = jnp.zeros_like(zero_ref)
        slabs = [pltpu.make_async_copy(zero_ref, xs_ref.at[pl.ds(z * ZERO_ROWS, ZERO_ROWS)], sem)
                 for z in range(zero_slab_ref.shape[0])]
        for z, slab in enumerate(slabs):
            pl.when(zero_slab_ref[z] != 0)(slab.start)
        for z, slab in enumerate(slabs):
            pl.when(zero_slab_ref[z] != 0)(slab.wait)

    def issue(t, carry):
        for k in range(TOP_K):
            pltpu.make_async_copy(x_ref.at[pl.ds(t, 1)],
                                  xs_ref.at[pl.ds(dest_ref[TOP_K * t + k], 1)], sem).start()
        return carry

    lax.fori_loop(0, tm, issue, 0, unroll=8)
    for k in range(TOP_K):
        pltpu.make_async_copy(x_ref, xs_ref.at[pl.ds(0, tm)], sem).wait()


def _dispatch(x, dest_flat, zero_slab, rows, tm=256):
    m, width = x.shape
    grid_spec = pltpu.PrefetchScalarGridSpec(
        num_scalar_prefetch=1, grid=(m // tm,),
        in_specs=[pl.BlockSpec((TOP_K * tm,), lambda i, zs: (i,), memory_space=pltpu.SMEM),
                  pl.BlockSpec((tm, width), lambda i, zs: (i, 0))],
        out_specs=pl.BlockSpec(memory_space=pl.ANY),
        scratch_shapes=[pltpu.VMEM((ZERO_ROWS, width), x.dtype), pltpu.SemaphoreType.DMA(())])
    return pl.pallas_call(
        functools.partial(_dispatch_kernel, tm=tm),
        grid_spec=grid_spec,
        out_shape=jax.ShapeDtypeStruct((rows, width), x.dtype),
        compiler_params=_params(("arbitrary",)),
        name="dispatch",
    )(zero_slab, dest_flat, x)


def _combine_ln_kernel(dest_ref, dest_next_ref, x_ref, gate_ref, g_ref, b_ref, y_ref, o_ref,
                       ybuf_ref, sems, *, tm):
    i, n = pl.program_id(0), pl.num_programs(0)
    slot = lax.rem(i, 2)

    def gather(d_ref, s):
        def issue(t, carry):
            for k in range(TOP_K):
                pltpu.make_async_copy(y_ref.at[pl.ds(d_ref[TOP_K * t + k], 1)],
                                      ybuf_ref.at[s, k, pl.ds(t, 1)], sems.at[s]).start()
            return carry

        lax.fori_loop(0, tm, issue, 0, unroll=8)

    @pl.when(i == 0)
    def _():
        gather(dest_ref, 0)

    @pl.when(i + 1 < n)
    def _():
        gather(dest_next_ref, 1 - slot)

    for k in range(TOP_K):
        pltpu.make_async_copy(y_ref.at[pl.ds(0, tm)], ybuf_ref.at[slot, k], sems.at[slot]).wait()
    gate = gate_ref[...]
    f = ybuf_ref[slot, 0] * gate[:, 0:1] + ybuf_ref[slot, 1] * gate[:, 1:2]
    o_ref[...] = _resid_layer_norm(x_ref[...], f, g_ref[...], b_ref[...])


def _combine_ln(x, y, dest_flat, gates, g, b, tm=256):
    m = x.shape[0]
    n_tiles = m // tm
    row = pl.BlockSpec((tm, D_MODEL), lambda i: (i, 0))
    vec = pl.BlockSpec((1, D_MODEL), lambda i: (0, 0))
    return pl.pallas_call(
        functools.partial(_combine_ln_kernel, tm=tm),
        grid=(n_tiles,),
        in_specs=[pl.BlockSpec((TOP_K * tm,), lambda i: (i,), memory_space=pltpu.SMEM),
                  pl.BlockSpec((TOP_K * tm,), lambda i: (jnp.minimum(i + 1, n_tiles - 1),),
                               memory_space=pltpu.SMEM),
                  row, pl.BlockSpec((tm, TOP_K), lambda i: (i, 0)), vec, vec,
                  pl.BlockSpec(memory_space=pl.ANY)],
        out_specs=row,
        out_shape=jax.ShapeDtypeStruct((m, D_MODEL), jnp.float32),
        scratch_shapes=[pltpu.VMEM((2, TOP_K, tm, D_MODEL), jnp.float32),
                        pltpu.SemaphoreType.DMA((2,))],
        compiler_params=_params(("arbitrary",)),
        name="combine_ln",
    )(dest_flat, dest_flat, x, gates, g.reshape(1, D_MODEL), b.reshape(1, D_MODEL), y)


def _rope_tables(seq):
    inv_freq = ROPE_THETA ** (-jnp.arange(0, ROPE_DIM, 2, dtype=jnp.float32) / ROPE_DIM)
    ang = jnp.arange(seq, dtype=jnp.float32)[:, None] * inv_freq[None, :]
    cos, sin = jnp.cos(ang), jnp.sin(ang)
    pad = HEAD_DIM - ROPE_DIM
    cos_full = jnp.concatenate([cos, cos, jnp.ones((seq, pad), jnp.float32)], axis=1)
    sin_full = jnp.concatenate([-sin, sin, jnp.zeros((seq, pad), jnp.float32)], axis=1)
    return cos_full, sin_full


def _mixer(x, x_bf, batch, w_in, conv_w, w_br_conv, w_br_attn, pool_mix, pool_scale, w_br_pool, w_o,
           ln_g, ln_b, rope_cos, rope_sin, packed=False):
    conv_cols = CONV_KERNEL * CONV_WIDTH
    pool_col = conv_cols + QKV_WIDTH
    conv_p = _project(x_bf, w_in, 0, conv_cols, 1536, jnp.bfloat16)
    qkv = _project(x_bf, w_in, conv_cols, QKV_WIDTH, QKV_WIDTH // 3, jnp.bfloat16,
                   rope=(rope_cos, rope_sin), rope_tiles=2)
    pool_gates_p = _project(x_bf, w_in, pool_col, POOL_WIDTH + N_GROUPS * D_MODEL, 1024, jnp.bfloat16)
    attn = _attention(qkv.reshape(batch, SEQ, QKV_WIDTH))
    attn = attn.reshape(batch * SEQ, GROUP_WIDTH)
    bf = lambda w: w.astype(jnp.bfloat16)
    return _mixer_tail(conv_p, pool_gates_p, attn, x, conv_w, pool_scale, ln_g, ln_b,
                       bf(w_br_conv), bf(w_br_attn), bf(w_br_pool), bf(pool_mix), bf(w_o), packed=packed)


def _dense_ffn(x, x_bf, w_gate, w_up, w_down, ln_g, ln_b, tm=1024, tf=512):
    n_blk = x.shape[0] // tm
    blk_exp = jnp.zeros((n_blk,), jnp.int32)
    blk_rows = jnp.full((n_blk,), tm, jnp.int32)
    n_used = jnp.full((1,), n_blk, jnp.int32)
    return _swiglu_blocks(x_bf, w_gate[None], w_up[None], w_down[None], blk_exp, blk_rows, n_used, tm, tf,
                          resid_ln=(x, ln_g, ln_b))


def _moe_ffn(x, x_packed, w_router, w_gate, w_up, w_down, ln_g, ln_b, tm=1024, tf=512):
    m = x.shape[0]
    idx, gates, rank, cnt = _route(x, w_router)
    counts = cnt[0, :N_EXPERTS].astype(jnp.int32)
    padded = (counts + tm - 1) // tm * tm
    pend = jnp.cumsum(padded)
    pstart = pend - padded
    dest_flat = (pstart[idx] + rank).reshape(-1)
    n_blk = m * TOP_K // tm + N_EXPERTS
    blk_start = jnp.arange(n_blk, dtype=jnp.int32) * tm
    blk_exp = jnp.minimum(jnp.sum(blk_start[:, None] >= pend[None, :], axis=1), N_EXPERTS - 1).astype(jnp.int32)
    blk_rows = jnp.clip(pstart[blk_exp] + counts[blk_exp] - blk_start, 0, tm).astype(jnp.int32)
    n_used = (pend[-1:] // tm).astype(jnp.int32)
    blk_rows = jnp.where(jnp.arange(n_blk) < n_used[0], blk_rows, 0)
    slab_end = (jnp.arange(n_blk * tm // ZERO_ROWS, dtype=jnp.int32) + 1) * ZERO_ROWS
    slab_exp = jnp.repeat(blk_exp, tm // ZERO_ROWS)
    zero_slab = (slab_end > pstart[slab_exp] + counts[slab_exp]).astype(jnp.int32)
    xs = _dispatch(x_packed, dest_flat, zero_slab, n_blk * tm)
    ys = _swiglu_blocks(xs, w_gate, w_up, w_down, blk_exp, blk_rows, n_used, tm, tf)
    return _combine_ln(x, ys, dest_flat, gates, ln_g, ln_b)


def kernel(x, l0_w_in, l0_conv_w, l0_w_br_conv, l0_w_br_attn, l0_pool_mix, l0_pool_scale, l0_w_br_pool, l0_w_o, l0_ln1_g, l0_ln1_b, l0_ffn_gate, l0_ffn_up, l0_ffn_down, l0_ln2_g, l0_ln2_b, l1_w_in, l1_conv_w, l1_w_br_conv, l1_w_br_attn, l1_pool_mix, l1_pool_scale, l1_w_br_pool, l1_w_o, l1_ln1_g, l1_ln1_b, l1_router, l1_exp_gate, l1_exp_up, l1_exp_down, l1_ln2_g, l1_ln2_b):
    batch, seq, d = x.shape
    assert seq == SEQ and d == D_MODEL
    rope_cos, rope_sin = _rope_tables(seq)
    x0 = x.reshape(batch * seq, d)
    x1, x1_bf = _mixer(x0, x0.astype(jnp.bfloat16), batch, l0_w_in, l0_conv_w, l0_w_br_conv, l0_w_br_attn,
                       l0_pool_mix, l0_pool_scale, l0_w_br_pool, l0_w_o, l0_ln1_g, l0_ln1_b, rope_cos, rope_sin)
    x2, x2_bf = _dense_ffn(x1, x1_bf, l0_ffn_gate, l0_ffn_up, l0_ffn_down, l0_ln2_g, l0_ln2_b)
    x3, x3_packed = _mixer(x2, x2_bf, batch, l1_w_in, l1_conv_w, l1_w_br_conv, l1_w_br_attn,
                           l1_pool_mix, l1_pool_scale, l1_w_br_pool, l1_w_o, l1_ln1_g, l1_ln1_b,
                           rope_cos, rope_sin, packed=True)
    x4 = _moe_ffn(x3, x3_packed, l1_router, l1_exp_gate, l1_exp_up, l1_exp_down, l1_ln2_g, l1_ln2_b)
    return x4.reshape(batch, seq, d)
```

```python
import functools

import jax
import jax.numpy as jnp
from jax import lax
from jax.experimental import pallas as pl
from jax.experimental.pallas import tpu as pltpu

D_MODEL = 2048
SEQ = 4096
CONV_WIDTH = D_MODEL // 2
CONV_KERNEL = 3
HEAD_DIM = 128
HEADS_PER_GROUP = 4
GROUP_WIDTH = HEADS_PER_GROUP * HEAD_DIM
ATTN_PATTERNS = ((128, 1), (512, 4), (2048, 16))
N_GROUPS = len(ATTN_PATTERNS)
ATTN_HEADS = N_GROUPS * HEADS_PER_GROUP
QKV_WIDTH = 3 * N_GROUPS * GROUP_WIDTH
ATTN_BLOCK = 128
ROPE_THETA = 500000.0
ROPE_DIM = HEAD_DIM // 4
POOL_WINDOWS = (2, 4, 8, 16)
POOL_WIDTH = D_MODEL // 2
POOL_GROUP_DIM = POOL_WIDTH // len(POOL_WINDOWS)
POOL_HALO = 16
CONV_HALO = 16
D_FF = 7 * D_MODEL // 2
N_EXPERTS = 8
TOP_K = 2
LN_EPS = 1e-5
DEPTH = 2
DEEPNORM_ALPHA = (2.0 * DEPTH) ** 0.25
NEG_INF = -1e30
LANES = 128

VMEM_LIMIT = 56 * 1024 * 1024


def _params(sem, vmem=VMEM_LIMIT):
    return pltpu.CompilerParams(dimension_semantics=sem, vmem_limit_bytes=vmem)


W_SLAB = 512
PROJ_CHUNK = 1024
MERGE_COLS = 512


def _rope(t, cos, sin):
    half = ROPE_DIM // 2
    lane = lax.broadcasted_iota(jnp.int32, t.shape, 1)
    partner = jnp.where(lane < half, pltpu.roll(t, HEAD_DIM - half, 1), pltpu.roll(t, half, 1))
    return t * cos + partner * sin


def _proj_kernel(*refs, n_slabs, rope_tiles):
    x_ref, w_refs = refs[0], refs[1:1 + n_slabs]
    o_ref, wbf_ref = refs[-2], refs[-1]

    @pl.when(pl.program_id(1) == 0)
    def _():
        for t, w_ref in enumerate(w_refs):
            wbf_ref[:, t * W_SLAB:(t + 1) * W_SLAB] = w_ref[...].astype(jnp.bfloat16)

    def plain():
        tn = o_ref.shape[1]
        for c0 in range(0, tn, PROJ_CHUNK):
            cols = slice(c0, min(c0 + PROJ_CHUNK, tn))
            o_ref[:, cols] = jnp.dot(x_ref[...], wbf_ref[:, cols],
                                     preferred_element_type=jnp.float32).astype(o_ref.dtype)

    if rope_tiles == 0:
        plain()
        return
    cos_ref, sin_ref = refs[1 + n_slabs], refs[2 + n_slabs]
    is_rope_tile = pl.program_id(0) < rope_tiles

    @pl.when(is_rope_tile)
    def _():
        cos, sin = cos_ref[...], sin_ref[...]
        for t in range(n_slabs):
            acc = jnp.dot(x_ref[...], wbf_ref[:, t * W_SLAB:(t + 1) * W_SLAB],
                          preferred_element_type=jnp.float32)
            for h in range(W_SLAB // HEAD_DIM):
                cols = slice(h * HEAD_DIM, (h + 1) * HEAD_DIM)
                out_cols = slice(t * W_SLAB + h * HEAD_DIM, t * W_SLAB + (h + 1) * HEAD_DIM)
                o_ref[:, out_cols] = _rope(acc[:, cols], cos, sin).astype(o_ref.dtype)

    pl.when(jnp.logical_not(is_rope_tile))(plain)


def _project(x_bf, w, col0, width, tn, out_dtype, tm=1024, rope=None, rope_tiles=0):
    m, k = x_bf.shape
    assert col0 % W_SLAB == 0 and tn % W_SLAB == 0 and width % tn == 0 and m % tm == 0
    n_slabs = tn // W_SLAB
    slab0 = col0 // W_SLAB
    w_specs = [pl.BlockSpec((k, W_SLAB), lambda j, i, t=t: (0, slab0 + j * n_slabs + t))
               for t in range(n_slabs)]
    rope_specs, rope_args = [], []
    if rope_tiles:
        tiles_per_seq = SEQ // tm
        rope_specs = [pl.BlockSpec((tm, LANES), lambda j, i: (i % tiles_per_seq, 0))] * 2
        rope_args = list(rope)
    return pl.pallas_call(
        functools.partial(_proj_kernel, n_slabs=n_slabs, rope_tiles=rope_tiles),
        grid=(width // tn, m // tm),
        in_specs=[pl.BlockSpec((tm, k), lambda j, i: (i, 0))] + w_specs + rope_specs,
        out_specs=pl.BlockSpec((tm, tn), lambda j, i: (i, j)),
        out_shape=jax.ShapeDtypeStruct((m, width), out_dtype),
        scratch_shapes=[pltpu.VMEM((k, tn), jnp.bfloat16)],
        compiler_params=_params(("arbitrary", "arbitrary")),
        name="in_proj",
    )(x_bf, *([w] * n_slabs), *rope_args)


def _strided(start, size, stride):
    return pl.ds(start, size) if stride == 1 else pl.ds(start, size, stride=stride)


def _attn_kernel(q0_ref, k0_ref, v0_ref, q1_ref, k1_ref, v1_ref, q2_ref, k2_ref, v2_ref,
                 o_ref, q_s, k_s, v_s, out_s, lse_s, *, chunk, chains):
    groups = ((q0_ref, k0_ref, v0_ref), (q1_ref, k1_ref, v1_ref), (q2_ref, k2_ref, v2_ref))
    n_keys = 2 * ATTN_BLOCK
    a = lax.broadcasted_iota(jnp.int32, (ATTN_BLOCK, n_keys), 0)
    kk = lax.broadcasted_iota(jnp.int32, (ATTN_BLOCK, n_keys), 1)
    scale = HEAD_DIM ** -0.5

    for g, (q_ref, k_ref, v_ref) in enumerate(groups):
        d = ATTN_PATTERNS[g][1]
        if d > 1:
            def stage(ci, carry, q_ref=q_ref, k_ref=k_ref, v_ref=v_ref):
                rows = pl.ds(pl.multiple_of(ci * chunk, chunk), chunk)
                q_s[rows, :] = q_ref[0, rows, :].astype(jnp.float32)
                k_s[rows, :] = k_ref[0, rows, :].astype(jnp.float32)
                v_s[rows, :] = v_ref[0, rows, :].astype(jnp.float32)
                return carry

            lax.fori_loop(0, SEQ // chunk, stage, 0)
            q_src, k_src, v_src = q_s, k_s, v_s
        else:
            q_src, k_src, v_src = q_ref.at[0], k_ref.at[0], v_ref.at[0]

        blocks_per_residue = SEQ // d // ATTN_BLOCK
        unroll = min(chains, blocks_per_residue)
        residues_per_body = chains // unroll
        for c0 in range(0, d, residues_per_body):
            def block(blk, carry, c0=c0, d=d, g=g, residues_per_body=residues_per_body,
                      q_src=q_src, k_src=k_src, v_src=v_src):
                base = pl.multiple_of(blk * (ATTN_BLOCK * d), ATTN_BLOCK)
                key_base = pl.multiple_of(jnp.maximum(base - ATTN_BLOCK * d, 0), ATTN_BLOCK)
                self_key = jnp.where(blk == 0, 0, ATTN_BLOCK)
                valid = (kk >= a + self_key - ATTN_BLOCK) & (kk <= a + self_key)
                for c in range(c0, c0 + residues_per_body):
                    qb = q_src[_strided(base + c, ATTN_BLOCK, d), :].astype(jnp.bfloat16)
                    kb = k_src[_strided(key_base + c, n_keys, d), :].astype(jnp.bfloat16)
                    vb = v_src[_strided(key_base + c, n_keys, d), :].astype(jnp.bfloat16)
                    s = lax.dot_general(qb, kb, (((1,), (1,)), ((), ())),
                                        preferred_element_type=jnp.float32) * scale
                    s = jnp.where(valid, s, NEG_INF)
                    m = jnp.max(s, axis=-1, keepdims=True)
                    p = jnp.exp(s - m)
                    l = jnp.sum(p, axis=-1, keepdims=True)
                    o = jnp.dot(p.astype(jnp.bfloat16), vb, preferred_element_type=jnp.float32)
                    rows = _strided(base + c, ATTN_BLOCK, d)
                    out_s[g, rows, :] = o / l
                    lse_s[g, rows, :] = jnp.broadcast_to(m + jnp.log(l), (ATTN_BLOCK, HEAD_DIM))
                return carry

            lax.fori_loop(0, blocks_per_residue, block, 0, unroll=unroll)

    def combine(ci, carry):
        rows = pl.ds(pl.multiple_of(ci * chunk, chunk), chunk)
        l0, l1, l2 = lse_s[0, rows, :], lse_s[1, rows, :], lse_s[2, rows, :]
        mx = jnp.maximum(jnp.maximum(l0, l1), l2)
        e0, e1, e2 = jnp.exp(l0 - mx), jnp.exp(l1 - mx), jnp.exp(l2 - mx)
        num = e0 * out_s[0, rows, :] + e1 * out_s[1, rows, :] + e2 * out_s[2, rows, :]
        o_ref[0, rows, :] = (num / (e0 + e1 + e2)).astype(o_ref.dtype)
        return carry

    lax.fori_loop(0, SEQ // chunk, combine, 0)


def _attention(qkv, chunk=256, chains=8):
    b, s, _ = qkv.shape
    assert s == SEQ

    def head_block(part, group):
        col0 = part * ATTN_HEADS + group * HEADS_PER_GROUP
        return pl.BlockSpec((1, s, HEAD_DIM), lambda bi, h: (bi, 0, col0 + h))

    in_specs = [head_block(part, group) for group in range(N_GROUPS) for part in range(3)]
    return pl.pallas_call(
        functools.partial(_attn_kernel, chunk=chunk, chains=chains),
        grid=(b, HEADS_PER_GROUP),
        in_specs=in_specs,
        out_specs=pl.BlockSpec((1, s, HEAD_DIM), lambda bi, h: (bi, 0, h)),
        out_shape=jax.ShapeDtypeStruct((b, s, GROUP_WIDTH), jnp.bfloat16),
        scratch_shapes=[pltpu.VMEM((s, HEAD_DIM), jnp.float32),
                        pltpu.VMEM((s, HEAD_DIM), jnp.float32),
                        pltpu.VMEM((s, HEAD_DIM), jnp.float32),
                        pltpu.VMEM((N_GROUPS, s, HEAD_DIM), jnp.float32),
                        pltpu.VMEM((N_GROUPS, s, HEAD_DIM), jnp.float32)],
        compiler_params=_params(("arbitrary", "arbitrary")),
        name="dilated_attn",
    )(*([qkv] * 9))


def _resid_layer_norm(x, f, g, b):
    z = DEEPNORM_ALPHA * x + f
    mu = jnp.mean(z, axis=-1, keepdims=True)
    zc = z - mu
    var = jnp.mean(zc * zc, axis=-1, keepdims=True)
    return zc * lax.rsqrt(var + LN_EPS) * g + b


def _pack_bf16_pairs(y):
    c = y.shape[1] // 2
    bits = pltpu.bitcast(y.astype(jnp.bfloat16).astype(jnp.float32), jnp.uint32)
    return bits[:, :c] | (bits[:, c:] >> 16)


def _unpack_bf16_pairs(w):
    hi = pltpu.bitcast(w & jnp.uint32(0xFFFF0000), jnp.float32).astype(jnp.bfloat16)
    lo = pltpu.bitcast(w << 16, jnp.float32).astype(jnp.bfloat16)
    return hi, lo


def _mixer_tail_kernel(cb_ref, cc_ref, ch_ref, cch_ref, chh_ref, pu_ref, puh_ref, attn_ref,
                       g0a_ref, g0b_ref, g1a_ref, g1b_ref, g2a_ref, g2b_ref, x_ref,
                       convw_ref, pscale_ref, lng_ref, lnb_ref,
                       wconv_ref, wattn_ref, wpool_ref, pmix_ref, wo_ref, o_ref, obf_ref,
                       *, tm, tiles_per_seq, packed):
    seq_tile = pl.program_id(0) % tiles_per_seq
    not_first = (seq_tile != 0).astype(jnp.float32)
    f32 = lambda ref: ref[...].astype(jnp.float32)

    u = f32(cc_ref) * f32(ch_ref)
    uh = f32(cch_ref) * f32(chh_ref) * not_first
    ext = jnp.concatenate([uh, u], axis=0)
    u1 = pltpu.roll(ext, 1, 0)[CONV_HALO:]
    u2 = pltpu.roll(ext, 2, 0)[CONV_HALO:]
    cw = convw_ref[...]
    conv = u2 * cw[0:1] + u1 * cw[1:2]
    conv = conv + u * cw[2:3]
    ya_in = (f32(cb_ref) * conv).astype(jnp.bfloat16)

    pu = f32(pu_ref)
    pext = jnp.concatenate([f32(puh_ref) * not_first, pu], axis=0)
    pos = seq_tile * tm + lax.broadcasted_iota(jnp.int32, (tm, 1), 0) + 1
    pscale = pscale_ref[...]
    mixed = []
    for gi, w in enumerate(POOL_WINDOWS):
        cols = slice(gi * POOL_GROUP_DIM, (gi + 1) * POOL_GROUP_DIM)
        run = pext[:, cols]
        step = 1
        while step < w:
            run = run + pltpu.roll(run, step, 0)
            step *= 2
        inv_cnt = 1.0 / jnp.minimum(pos, w).astype(jnp.float32)
        pooled = run[POOL_HALO:] * inv_cnt - pu[:, cols]
        mix = jnp.dot(pooled.astype(jnp.bfloat16), pmix_ref[gi], preferred_element_type=jnp.float32)
        mixed.append((mix * pscale[:, cols]).astype(jnp.bfloat16))
    yc_in = jnp.concatenate(mixed, axis=1)

    pieces = []
    for hc, (ga, gb, gc) in enumerate(((g0a_ref, g1a_ref, g2a_ref), (g0b_ref, g1b_ref, g2b_ref))):
        for q in range(CONV_WIDTH // MERGE_COLS):
            gcols = slice(q * MERGE_COLS, (q + 1) * MERGE_COLS)
            cols = slice(hc * CONV_WIDTH + q * MERGE_COLS, hc * CONV_WIDTH + (q + 1) * MERGE_COLS)
            sig = lambda ref: jax.nn.sigmoid(ref[:, gcols].astype(jnp.float32))
            part = sig(ga) * jnp.dot(ya_in, wconv_ref[:, cols], preferred_element_type=jnp.float32)
            part = part + sig(gb) * jnp.dot(attn_ref[...], wattn_ref[:, cols],
                                            preferred_element_type=jnp.float32)
            part = part + sig(gc) * jnp.dot(yc_in, wpool_ref[:, cols], preferred_element_type=jnp.float32)
            pieces.append(part.astype(jnp.bfloat16))
    merged = jnp.concatenate(pieces, axis=1)

    f = jnp.dot(merged, wo_ref[...], preferred_element_type=jnp.float32)
    y = _resid_layer_norm(x_ref[...], f, lng_ref[...], lnb_ref[...])
    o_ref[...] = y
    obf_ref[...] = _pack_bf16_pairs(y) if packed else y.astype(jnp.bfloat16)


def _mixer_tail(conv_p, pool_gates_p, attn, x, conv_w, pool_scale, ln_g, ln_b,
                w_conv_bf, w_attn_bf, w_pool_bf, pool_mix_bf, w_o_bf, packed=False, tm=256):
    m = conv_p.shape[0]
    gate_blocks = N_GROUPS * D_MODEL // CONV_WIDTH
    tiles_per_seq = SEQ // tm
    row = lambda width, col: pl.BlockSpec((tm, width), lambda i: (i, col))

    def halo(rows, col):
        per_tile = tm // rows
        return pl.BlockSpec((rows, CONV_WIDTH), lambda i: (jnp.maximum(i * per_tile - 1, 0), col))

    full = lambda shape: pl.BlockSpec(shape, lambda i: (0,) * len(shape), pipeline_mode=pl.Buffered(1))
    in_specs = [row(CONV_WIDTH, 0), row(CONV_WIDTH, 1), row(CONV_WIDTH, 2),
                halo(CONV_HALO, 1), halo(CONV_HALO, 2),
                row(POOL_WIDTH, 0), halo(POOL_HALO, 0),
                row(GROUP_WIDTH, 0)]
    in_specs += [row(CONV_WIDTH, 1 + gb) for gb in range(gate_blocks)]
    in_specs += [row(D_MODEL, 0),
                 full((CONV_KERNEL, CONV_WIDTH)), full((1, POOL_WIDTH)), full((1, D_MODEL)), full((1, D_MODEL)),
                 full((CONV_WIDTH, D_MODEL)), full((GROUP_WIDTH, D_MODEL)), full((POOL_WIDTH, D_MODEL)),
                 full((len(POOL_WINDOWS), POOL_GROUP_DIM, POOL_GROUP_DIM)), full((D_MODEL, D_MODEL))]
    return pl.pallas_call(
        functools.partial(_mixer_tail_kernel, tm=tm, tiles_per_seq=tiles_per_seq, packed=packed),
        grid=(m // tm,),
        in_specs=in_specs,
        out_specs=[row(D_MODEL, 0), row(D_MODEL // 2 if packed else D_MODEL, 0)],
        out_shape=[jax.ShapeDtypeStruct((m, D_MODEL), jnp.float32),
                   jax.ShapeDtypeStruct((m, D_MODEL // 2), jnp.uint32) if packed
                   else jax.ShapeDtypeStruct((m, D_MODEL), jnp.bfloat16)],
        compiler_params=_params(("arbitrary",)),
        name="mixer_tail",
    )(conv_p, conv_p, conv_p, conv_p, conv_p, pool_gates_p, pool_gates_p, attn,
      *([pool_gates_p] * gate_blocks), x,
      conv_w, pool_scale.reshape(1, POOL_WIDTH), ln_g.reshape(1, D_MODEL), ln_b.reshape(1, D_MODEL),
      w_conv_bf, w_attn_bf, w_pool_bf, pool_mix_bf, w_o_bf)


def _route_tile(y, w_ref, idx_ref, gate_ref, rank_ref, cnt_ref, carry_ref, *, tm):
    @pl.when(pl.program_id(0) == 0)
    def _():
        carry_ref[...] = jnp.zeros_like(carry_ref)

    w = w_ref[...]
    y_hi, w_hi = y.astype(jnp.bfloat16), w.astype(jnp.bfloat16)
    y_lo = (y - y_hi.astype(jnp.float32)).astype(jnp.bfloat16)
    w_lo = (w - w_hi.astype(jnp.float32)).astype(jnp.bfloat16)
    logits = (jnp.dot(y_hi, w_hi, preferred_element_type=jnp.float32)
              + jnp.dot(y_lo, w_hi, preferred_element_type=jnp.float32)
              + jnp.dot(y_hi, w_lo, preferred_element_type=jnp.float32))
    lane = lax.broadcasted_iota(jnp.int32, logits.shape, 1)
    logits = jnp.where(lane < N_EXPERTS, logits, -jnp.inf)
    m1 = jnp.max(logits, axis=-1, keepdims=True)
    i1 = jnp.min(jnp.where(logits == m1, lane, LANES), axis=-1, keepdims=True)
    rest = jnp.where(lane == i1, -jnp.inf, logits)
    m2 = jnp.max(rest, axis=-1, keepdims=True)
    i2 = jnp.min(jnp.where(rest == m2, lane, LANES), axis=-1, keepdims=True)
    e = jnp.exp(m2 - m1)
    den = 1.0 + e
    idx_ref[:, 0:1] = i1
    idx_ref[:, 1:2] = i2
    gate_ref[:, 0:1] = 1.0 / den
    gate_ref[:, 1:2] = e / den

    oh1 = (lane == i1).astype(jnp.float32)
    oh2 = (lane == i2).astype(jnp.float32)
    both = oh1 + oh2
    r = lax.broadcasted_iota(jnp.int32, (tm, tm), 0)
    c = lax.broadcasted_iota(jnp.int32, (tm, tm), 1)
    strict_lower = (r > c).astype(jnp.bfloat16)
    before = jnp.dot(strict_lower, both.astype(jnp.bfloat16), preferred_element_type=jnp.float32)
    before = before + carry_ref[...]
    rank_ref[:, 0:1] = jnp.sum(oh1 * before, axis=-1, keepdims=True).astype(jnp.int32)
    rank_ref[:, 1:2] = jnp.sum(oh2 * before, axis=-1, keepdims=True).astype(jnp.int32)
    carry_ref[...] += jnp.sum(both, axis=0, keepdims=True)
    cnt_ref[...] = carry_ref[...]


def _route_kernel(x_ref, w_ref, idx_ref, gate_ref, rank_ref, cnt_ref, carry_ref, *, tm):
    _route_tile(x_ref[...], w_ref, idx_ref, gate_ref, rank_ref, cnt_ref, carry_ref, tm=tm)


def _route(x, w_router, tm=512):
    m = x.shape[0]
    w_pad = jnp.pad(w_router, ((0, 0), (0, LANES - N_EXPERTS)))
    pair = pl.BlockSpec((tm, TOP_K), lambda i: (i, 0))
    return pl.pallas_call(
        functools.partial(_route_kernel, tm=tm),
        grid=(m // tm,),
        in_specs=[pl.BlockSpec((tm, D_MODEL), lambda i: (i, 0)),
                  pl.BlockSpec((D_MODEL, LANES), lambda i: (0, 0))],
        out_specs=[pair, pair, pair, pl.BlockSpec((1, LANES), lambda i: (0, 0))],
        out_shape=[jax.ShapeDtypeStruct((m, TOP_K), jnp.int32),
                   jax.ShapeDtypeStruct((m, TOP_K), jnp.float32),
                   jax.ShapeDtypeStruct((m, TOP_K), jnp.int32),
                   jax.ShapeDtypeStruct((1, LANES), jnp.float32)],
        scratch_shapes=[pltpu.VMEM((1, LANES), jnp.float32)],
        compiler_params=_params(("arbitrary",)),
        name="router",
    )(x, w_pad)


def _swiglu_kernel(blk_exp_ref, blk_rows_ref, n_used_ref, x_ref, wg_ref, wu_ref, wd_ref, *refs,
                   tm, sub, last_j, resid_ln):
    del blk_exp_ref, n_used_ref
    i, j = pl.program_id(0), pl.program_id(1)
    if resid_ln:
        res_hbm, g_ref, b_ref, o_ref, obf_ref, res_ref, res_sem = refs
        res_copy = pltpu.make_async_copy(res_hbm.at[pl.ds(pl.multiple_of(i * tm, tm), tm)], res_ref, res_sem)
        xbf_ref = x_ref
    else:
        o_ref, xbf_ref = refs
    n_sub = (blk_rows_ref[i] + sub - 1) // sub

    @pl.when(j == 0)
    def _():
        if resid_ln:
            res_copy.start()
        else:
            half = x_ref.shape[1]
            xbf_ref[:, :half], xbf_ref[:, half:] = _unpack_bf16_pairs(x_ref[...])
        o_ref[...] = jnp.zeros_like(o_ref)

    for v in range(1, tm // sub + 1):
        rows = v * sub

        @pl.when(n_sub == v)
        def _(rows=rows):
            x = xbf_ref[0:rows, :]
            gate = jnp.dot(x, wg_ref[0].astype(jnp.bfloat16), preferred_element_type=jnp.float32)
            up = jnp.dot(x, wu_ref[0].astype(jnp.bfloat16), preferred_element_type=jnp.float32)
            h = (gate * jax.nn.sigmoid(gate) * up).astype(jnp.bfloat16)
            o_ref[0:rows, :] += jnp.dot(h, wd_ref[0].astype(jnp.bfloat16),
                                        preferred_element_type=jnp.float32)

    if resid_ln:
        @pl.when(j == last_j)
        def _():
            res_copy.wait()

            def norm_rows(ci, carry):
                rows = pl.ds(pl.multiple_of(ci * sub, sub), sub)
                y = _resid_layer_norm(res_ref[rows, :], o_ref[rows, :], g_ref[...], b_ref[...])
                o_ref[rows, :] = y
                obf_ref[rows, :] = y.astype(jnp.bfloat16)
                return carry

            lax.fori_loop(0, tm // sub, norm_rows, 0)


def _swiglu_blocks(x, w_gate, w_up, w_down, blk_exp, blk_rows, n_used, tm, tf, sub=256, resid_ln=None):
    r = x.shape[0]
    nf = D_FF // tf
    last = nf - 1

    def live(i, nu):
        return jnp.maximum(jnp.minimum(i, nu[0] - 1), 0)

    def jj(i, j, nu):
        return jnp.where(i < nu[0], j, last)

    once = pl.Buffered(1)
    x_spec = pl.BlockSpec((tm, x.shape[1]), lambda i, j, be, br, nu: (live(i, nu), 0),
                          pipeline_mode=once if resid_ln is not None else None)
    row_out = pl.BlockSpec((tm, D_MODEL), lambda i, j, be, br, nu: (i, 0), pipeline_mode=once)
    vec = pl.BlockSpec((1, D_MODEL), lambda i, j, be, br, nu: (0, 0))
    in_specs = [x_spec,
                pl.BlockSpec((1, D_MODEL, tf), lambda i, j, be, br, nu: (be[live(i, nu)], 0, jj(i, j, nu))),
                pl.BlockSpec((1, D_MODEL, tf), lambda i, j, be, br, nu: (be[live(i, nu)], 0, jj(i, j, nu))),
                pl.BlockSpec((1, tf, D_MODEL), lambda i, j, be, br, nu: (be[live(i, nu)], jj(i, j, nu), 0))]
    args = [x, w_gate, w_up, w_down]
    out_specs = row_out
    out_shape = jax.ShapeDtypeStruct((r, D_MODEL), jnp.float32)
    if resid_ln is not None:
        assert x.dtype == jnp.bfloat16
        res, g, b = resid_ln
        in_specs += [pl.BlockSpec(memory_space=pl.ANY), vec, vec]
        args += [res, g.reshape(1, D_MODEL), b.reshape(1, D_MODEL)]
        out_specs = [row_out, row_out]
        out_shape = [out_shape, jax.ShapeDtypeStruct((r, D_MODEL), jnp.bfloat16)]
        scratch = [pltpu.VMEM((tm, D_MODEL), jnp.float32), pltpu.SemaphoreType.DMA(())]
    else:
        assert x.dtype == jnp.uint32
        scratch = [pltpu.VMEM((tm, D_MODEL), jnp.bfloat16)]
    grid_spec = pltpu.PrefetchScalarGridSpec(
        num_scalar_prefetch=3, grid=(r // tm, nf),
        in_specs=in_specs, out_specs=out_specs, scratch_shapes=scratch)
    return pl.pallas_call(
        functools.partial(_swiglu_kernel, tm=tm, sub=sub, last_j=last, resid_ln=resid_ln is not None),
        grid_spec=grid_spec,
        out_shape=out_shape,
        compiler_params=_params(("arbitrary", "arbitrary")),
        name="swiglu_blocks",
    )(blk_exp, blk_rows, n_used, *args)


ZERO_ROWS = 256


def _dispatch_kernel(zero_slab_ref, dest_ref, x_ref, xs_ref, zero_ref, sem, *, tm):
    @pl.when(pl.program_id(0) == 0)
    def _():
        zero_ref[...] = jnp.zeros_like(zero_ref)
        slabs = [pltpu.make_async_copy(zero_ref, xs_ref.at[pl.ds(z * ZERO_ROWS, ZERO_ROWS)], sem)
                 for z in range(zero_slab_ref.shape[0])]
        for z, slab in enumerate(slabs):
            pl.when(zero_slab_ref[z] != 0)(slab.start)
        for z, slab in enumerate(slabs):
            pl.when(zero_slab_ref[z] != 0)(slab.wait)

    def issue(t, carry):
        for k in range(TOP_K):
            pltpu.make_async_copy(x_ref.at[pl.ds(t, 1)],
                                  xs_ref.at[pl.ds(dest_ref[TOP_K * t + k], 1)], sem).start(priority=k % 2)
        return carry

    lax.fori_loop(0, tm, issue, 0, unroll=8)
    for k in range(TOP_K):
        pltpu.make_async_copy(x_ref, xs_ref.at[pl.ds(0, tm)], sem).wait()


def _dispatch(x, dest_flat, zero_slab, rows, tm=256):
    m, width = x.shape
    grid_spec = pltpu.PrefetchScalarGridSpec(
        num_scalar_prefetch=1, grid=(m // tm,),
        in_specs=[pl.BlockSpec((TOP_K * tm,), lambda i, zs: (i,), memory_space=pltpu.SMEM),
                  pl.BlockSpec((tm, width), lambda i, zs: (i, 0))],
        out_specs=pl.BlockSpec(memory_space=pl.ANY),
        scratch_shapes=[pltpu.VMEM((ZERO_ROWS, width), x.dtype), pltpu.SemaphoreType.DMA(())])
    return pl.pallas_call(
        functools.partial(_dispatch_kernel, tm=tm),
        grid_spec=grid_spec,
        out_shape=jax.ShapeDtypeStruct((rows, width), x.dtype),
        compiler_params=_params(("arbitrary",)),
        name="dispatch",
    )(zero_slab, dest_flat, x)


def _combine_ln_kernel(dest_ref, dest_next_ref, x_ref, gate_ref, g_ref, b_ref, y_ref, o_ref,
                       ybuf_ref, sems, *, tm):
    i, n = pl.program_id(0), pl.num_programs(0)
    slot = lax.rem(i, 2)

    def gather(d_ref, s):
        def issue(t, carry):
            for k in range(TOP_K):
                pltpu.make_async_copy(y_ref.at[pl.ds(d_ref[TOP_K * t + k], 1)],
                                      ybuf_ref.at[s, k, pl.ds(t, 1)], sems.at[s]).start(priority=k % 2)
            return carry

        lax.fori_loop(0, tm, issue, 0, unroll=8)

    @pl.when(i == 0)
    def _():
        gather(dest_ref, 0)

    @pl.when(i + 1 < n)
    def _():
        gather(dest_next_ref, 1 - slot)

    for k in range(TOP_K):
        pltpu.make_async_copy(y_ref.at[pl.ds(0, tm)], ybuf_ref.at[slot, k], sems.at[slot]).wait()
    gate = gate_ref[...]
    f = ybuf_ref[slot, 0] * gate[:, 0:1] + ybuf_ref[slot, 1] * gate[:, 1:2]
    o_ref[...] = _resid_layer_norm(x_ref[...], f, g_ref[...], b_ref[...])


def _combine_ln(x, y, dest_flat, gates, g, b, tm=256):
    m = x.shape[0]
    n_tiles = m // tm
    row = pl.BlockSpec((tm, D_MODEL), lambda i: (i, 0))
    vec = pl.BlockSpec((1, D_MODEL), lambda i: (0, 0))
    return pl.pallas_call(
        functools.partial(_combine_ln_kernel, tm=tm),
        grid=(n_tiles,),
        in_specs=[pl.BlockSpec((TOP_K * tm,), lambda i: (i,), memory_space=pltpu.SMEM),
                  pl.BlockSpec((TOP_K * tm,), lambda i: (jnp.minimum(i + 1, n_tiles - 1),),
                               memory_space=pltpu.SMEM),
                  row, pl.BlockSpec((tm, TOP_K), lambda i: (i, 0)), vec, vec,
                  pl.BlockSpec(memory_space=pl.ANY)],
        out_specs=row,
        out_shape=jax.ShapeDtypeStruct((m, D_MODEL), jnp.float32),
        scratch_shapes=[pltpu.VMEM((2, TOP_K, tm, D_MODEL), jnp.float32),
                        pltpu.SemaphoreType.DMA((2,))],
        compiler_params=_params(("arbitrary",)),
        name="combine_ln",
    )(dest_flat, dest_flat, x, gates, g.reshape(1, D_MODEL), b.reshape(1, D_MODEL), y)


def _rope_tables(seq):
    inv_freq = ROPE_THETA ** (-jnp.arange(0, ROPE_DIM, 2, dtype=jnp.float32) / ROPE_DIM)
    ang = jnp.arange(seq, dtype=jnp.float32)[:, None] * inv_freq[None, :]
    cos, sin = jnp.cos(ang), jnp.sin(ang)
    pad = HEAD_DIM - ROPE_DIM
    cos_full = jnp.concatenate([cos, cos, jnp.ones((seq, pad), jnp.float32)], axis=1)
    sin_full = jnp.concatenate([-sin, sin, jnp.zeros((seq, pad), jnp.float32)], axis=1)
    return cos_full, sin_full


def _mixer(x, x_bf, batch, w_in, conv_w, w_br_conv, w_br_attn, pool_mix, pool_scale, w_br_pool, w_o,
           ln_g, ln_b, rope_cos, rope_sin, packed=False):
    conv_cols = CONV_KERNEL * CONV_WIDTH
    pool_col = conv_cols + QKV_WIDTH
    conv_p = _project(x_bf, w_in, 0, conv_cols, 1536, jnp.bfloat16)
    qkv = _project(x_bf, w_in, conv_cols, QKV_WIDTH, QKV_WIDTH // 3, jnp.bfloat16,
                   rope=(rope_cos, rope_sin), rope_tiles=2)
    pool_gates_p = _project(x_bf, w_in, pool_col, POOL_WIDTH + N_GROUPS * D_MODEL, 1024, jnp.bfloat16)
    attn = _attention(qkv.reshape(batch, SEQ, QKV_WIDTH))
    attn = attn.reshape(batch * SEQ, GROUP_WIDTH)
    bf = lambda w: w.astype(jnp.bfloat16)
    return _mixer_tail(conv_p, pool_gates_p, attn, x, conv_w, pool_scale, ln_g, ln_b,
                       bf(w_br_conv), bf(w_br_attn), bf(w_br_pool), bf(pool_mix), bf(w_o), packed=packed)


def _dense_ffn(x, x_bf, w_gate, w_up, w_down, ln_g, ln_b, tm=1024, tf=512):
    n_blk = x.shape[0] // tm
    blk_exp = jnp.zeros((n_blk,), jnp.int32)
    blk_rows = jnp.full((n_blk,), tm, jnp.int32)
    n_used = jnp.full((1,), n_blk, jnp.int32)
    return _swiglu_blocks(x_bf, w_gate[None], w_up[None], w_down[None], blk_exp, blk_rows, n_used, tm, tf,
                          resid_ln=(x, ln_g, ln_b))


def _moe_ffn(x, x_packed, w_router, w_gate, w_up, w_down, ln_g, ln_b, tm=1024, tf=512):
    m = x.shape[0]
    idx, gates, rank, cnt = _route(x, w_router)
    counts = cnt[0, :N_EXPERTS].astype(jnp.int32)
    padded = (counts + tm - 1) // tm * tm
    pend = jnp.cumsum(padded)
    pstart = pend - padded
    dest_flat = (pstart[idx] + rank).reshape(-1)
    n_blk = m * TOP_K // tm + N_EXPERTS
    blk_start = jnp.arange(n_blk, dtype=jnp.int32) * tm
    blk_exp = jnp.minimum(jnp.sum(blk_start[:, None] >= pend[None, :], axis=1), N_EXPERTS - 1).astype(jnp.int32)
    blk_rows = jnp.clip(pstart[blk_exp] + counts[blk_exp] - blk_start, 0, tm).astype(jnp.int32)
    n_used = (pend[-1:] // tm).astype(jnp.int32)
    blk_rows = jnp.where(jnp.arange(n_blk) < n_used[0], blk_rows, 0)
    slab_end = (jnp.arange(n_blk * tm // ZERO_ROWS, dtype=jnp.int32) + 1) * ZERO_ROWS
    slab_exp = jnp.repeat(blk_exp, tm // ZERO_ROWS)
    zero_slab = (slab_end > pstart[slab_exp] + counts[slab_exp]).astype(jnp.int32)
    xs = _dispatch(x_packed, dest_flat, zero_slab, n_blk * tm)
    ys = _swiglu_blocks(xs, w_gate, w_up, w_down, blk_exp, blk_rows, n_used, tm, tf)
    return _combine_ln(x, ys, dest_flat, gates, ln_g, ln_b)


def kernel(x, l0_w_in, l0_conv_w, l0_w_br_conv, l0_w_br_attn, l0_pool_mix, l0_pool_scale, l0_w_br_pool, l0_w_o, l0_ln1_g, l0_ln1_b, l0_ffn_gate, l0_ffn_up, l0_ffn_down, l0_ln2_g, l0_ln2_b, l1_w_in, l1_conv_w, l1_w_br_conv, l1_w_br_attn, l1_pool_mix, l1_pool_scale, l1_w_br_pool, l1_w_o, l1_ln1_g, l1_ln1_b, l1_router, l1_exp_gate, l1_exp_up, l1_exp_down, l1_ln2_g, l1_ln2_b):
    batch, seq, d = x.shape
    assert seq == SEQ and d == D_MODEL
    rope_cos, rope_sin = _rope_tables(seq)
    x0 = x.reshape(batch * seq, d)
    x1, x1_bf = _mixer(x0, x0.astype(jnp.bfloat16), batch, l0_w_in, l0_conv_w, l0_w_br_conv, l0_w_br_attn,
                       l0_pool_mix, l0_pool_scale, l0_w_br_pool, l0_w_o, l0_ln1_g, l0_ln1_b, rope_cos, rope_sin)
    x2, x2_bf = _dense_ffn(x1, x1_bf, l0_ffn_gate, l0_ffn_up, l0_ffn_down, l0_ln2_g, l0_ln2_b)
    x3, x3_packed = _mixer(x2, x2_bf, batch, l1_w_in, l1_conv_w, l1_w_br_conv, l1_w_br_attn,
                           l1_pool_mix, l1_pool_scale, l1_w_br_pool, l1_w_o, l1_ln1_g, l1_ln1_b,
                           rope_cos, rope_sin, packed=True)
    x4 = _moe_ffn(x3, x3_packed, l1_router, l1_exp_gate, l1_exp_up, l1_exp_down, l1_ln2_g, l1_ln2_b)
    return x4.reshape(batch, seq, d)
```

```python
import functools

import jax
import jax.numpy as jnp
from jax import lax
from jax.experimental import pallas as pl
from jax.experimental.pallas import tpu as pltpu

D_MODEL = 2048
SEQ = 4096
CONV_WIDTH = D_MODEL // 2
CONV_KERNEL = 3
HEAD_DIM = 128
HEADS_PER_GROUP = 4
GROUP_WIDTH = HEADS_PER_GROUP * HEAD_DIM
ATTN_PATTERNS = ((128, 1), (512, 4), (2048, 16))
N_GROUPS = len(ATTN_PATTERNS)
ATTN_HEADS = N_GROUPS * HEADS_PER_GROUP
QKV_WIDTH = 3 * N_GROUPS * GROUP_WIDTH
ATTN_BLOCK = 128
ROPE_THETA = 500000.0
ROPE_DIM = HEAD_DIM // 4
POOL_WINDOWS = (2, 4, 8, 16)
POOL_WIDTH = D_MODEL // 2
POOL_GROUP_DIM = POOL_WIDTH // len(POOL_WINDOWS)
POOL_HALO = 16
CONV_HALO = 16
D_FF = 7 * D_MODEL // 2
N_EXPERTS = 8
TOP_K = 2
LN_EPS = 1e-5
DEPTH = 2
DEEPNORM_ALPHA = (2.0 * DEPTH) ** 0.25
NEG_INF = -1e30
LANES = 128

VMEM_LIMIT = 56 * 1024 * 1024


def _params(sem, vmem=VMEM_LIMIT):
    return pltpu.CompilerParams(dimension_semantics=sem, vmem_limit_bytes=vmem)


W_SLAB = 512
PROJ_CHUNK = 1024
MERGE_COLS = 512


def _rope(t, cos, sin):
    half = ROPE_DIM // 2
    lane = lax.broadcasted_iota(jnp.int32, t.shape, 1)
    partner = jnp.where(lane < half, pltpu.roll(t, HEAD_DIM - half, 1), pltpu.roll(t, half, 1))
    return t * cos + partner * sin


def _proj_kernel(*refs, n_slabs, rope_tiles):
    x_ref, w_refs = refs[0], refs[1:1 + n_slabs]
    o_ref, wbf_ref = refs[-2], refs[-1]

    @pl.when(pl.program_id(1) == 0)
    def _():
        for t, w_ref in enumerate(w_refs):
            wbf_ref[:, t * W_SLAB:(t + 1) * W_SLAB] = w_ref[...].astype(jnp.bfloat16)

    def plain():
        tn = o_ref.shape[1]
        for c0 in range(0, tn, PROJ_CHUNK):
            cols = slice(c0, min(c0 + PROJ_CHUNK, tn))
            o_ref[:, cols] = jnp.dot(x_ref[...], wbf_ref[:, cols],
                                     preferred_element_type=jnp.float32).astype(o_ref.dtype)

    if rope_tiles == 0:
        plain()
        return
    cos_ref, sin_ref = refs[1 + n_slabs], refs[2 + n_slabs]
    is_rope_tile = pl.program_id(0) < rope_tiles

    @pl.when(is_rope_tile)
    def _():
        cos, sin = cos_ref[...], sin_ref[...]
        for t in range(n_slabs):
            acc = jnp.dot(x_ref[...], wbf_ref[:, t * W_SLAB:(t + 1) * W_SLAB],
                          preferred_element_type=jnp.float32)
            for h in range(W_SLAB // HEAD_DIM):
                cols = slice(h * HEAD_DIM, (h + 1) * HEAD_DIM)
                out_cols = slice(t * W_SLAB + h * HEAD_DIM, t * W_SLAB + (h + 1) * HEAD_DIM)
                o_ref[:, out_cols] = _rope(acc[:, cols], cos, sin).astype(o_ref.dtype)

    pl.when(jnp.logical_not(is_rope_tile))(plain)


def _project(x_bf, w, col0, width, tn, out_dtype, tm=1024, rope=None, rope_tiles=0):
    m, k = x_bf.shape
    assert col0 % W_SLAB == 0 and tn % W_SLAB == 0 and width % tn == 0 and m % tm == 0
    n_slabs = tn // W_SLAB
    slab0 = col0 // W_SLAB
    w_specs = [pl.BlockSpec((k, W_SLAB), lambda j, i, t=t: (0, slab0 + j * n_slabs + t))
               for t in range(n_slabs)]
    rope_specs, rope_args = [], []
    if rope_tiles:
        tiles_per_seq = SEQ // tm
        rope_specs = [pl.BlockSpec((tm, LANES), lambda j, i: (i % tiles_per_seq, 0))] * 2
        rope_args = list(rope)
    return pl.pallas_call(
        functools.partial(_proj_kernel, n_slabs=n_slabs, rope_tiles=rope_tiles),
        grid=(width // tn, m // tm),
        in_specs=[pl.BlockSpec((tm, k), lambda j, i: (i, 0))] + w_specs + rope_specs,
        out_specs=pl.BlockSpec((tm, tn), lambda j, i: (i, j)),
        out_shape=jax.ShapeDtypeStruct((m, width), out_dtype),
        scratch_shapes=[pltpu.VMEM((k, tn), jnp.bfloat16)],
        compiler_params=_params(("arbitrary", "arbitrary")),
        name="in_proj",
    )(x_bf, *([w] * n_slabs), *rope_args)


def _strided(start, size, stride):
    return pl.ds(start, size) if stride == 1 else pl.ds(start, size, stride=stride)


def _attn_kernel(q0_ref, k0_ref, v0_ref, q1_ref, k1_ref, v1_ref, q2_ref, k2_ref, v2_ref,
                 o_ref, q_s, k_s, v_s, out_s, lse_s, *, chunk, chains):
    groups = ((q0_ref, k0_ref, v0_ref), (q1_ref, k1_ref, v1_ref), (q2_ref, k2_ref, v2_ref))
    n_keys = 2 * ATTN_BLOCK
    a = lax.broadcasted_iota(jnp.int32, (ATTN_BLOCK, n_keys), 0)
    kk = lax.broadcasted_iota(jnp.int32, (ATTN_BLOCK, n_keys), 1)
    scale = HEAD_DIM ** -0.5

    for g, (q_ref, k_ref, v_ref) in enumerate(groups):
        d = ATTN_PATTERNS[g][1]
        if d > 1:
            def stage(ci, carry, q_ref=q_ref, k_ref=k_ref, v_ref=v_ref):
                rows = pl.ds(pl.multiple_of(ci * chunk, chunk), chunk)
                q_s[rows, :] = q_ref[0, rows, :].astype(jnp.float32)
                k_s[rows, :] = k_ref[0, rows, :].astype(jnp.float32)
                v_s[rows, :] = v_ref[0, rows, :].astype(jnp.float32)
                return carry

            lax.fori_loop(0, SEQ // chunk, stage, 0)
            q_src, k_src, v_src = q_s, k_s, v_s
        else:
            q_src, k_src, v_src = q_ref.at[0], k_ref.at[0], v_ref.at[0]

        blocks_per_residue = SEQ // d // ATTN_BLOCK
        unroll = min(chains, blocks_per_residue)
        residues_per_body = chains // unroll
        for c0 in range(0, d, residues_per_body):
            def block(blk, carry, c0=c0, d=d, g=g, residues_per_body=residues_per_body,
                      q_src=q_src, k_src=k_src, v_src=v_src):
                base = pl.multiple_of(blk * (ATTN_BLOCK * d), ATTN_BLOCK)
                key_base = pl.multiple_of(jnp.maximum(base - ATTN_BLOCK * d, 0), ATTN_BLOCK)
                self_key = jnp.where(blk == 0, 0, ATTN_BLOCK)
                valid = (kk >= a + self_key - ATTN_BLOCK) & (kk <= a + self_key)
                for c in range(c0, c0 + residues_per_body):
                    qb = q_src[_strided(base + c, ATTN_BLOCK, d), :].astype(jnp.bfloat16)
                    kb = k_src[_strided(key_base + c, n_keys, d), :].astype(jnp.bfloat16)
                    vb = v_src[_strided(key_base + c, n_keys, d), :].astype(jnp.bfloat16)
                    s = lax.dot_general(qb, kb, (((1,), (1,)), ((), ())),
                                        preferred_element_type=jnp.float32) * scale
                    s = jnp.where(valid, s, NEG_INF)
                    m = jnp.max(s, axis=-1, keepdims=True)
                    p = jnp.exp(s - m)
                    l = jnp.sum(p, axis=-1, keepdims=True)
                    o = jnp.dot(p.astype(jnp.bfloat16), vb, preferred_element_type=jnp.float32)
                    rows = _strided(base + c, ATTN_BLOCK, d)
                    out_s[g, rows, :] = o / l
                    lse_s[g, rows, :] = jnp.broadcast_to(m + jnp.log(l), (ATTN_BLOCK, HEAD_DIM))
                return carry

            lax.fori_loop(0, blocks_per_residue, block, 0, unroll=unroll)

    def combine(ci, carry):
        rows = pl.ds(pl.multiple_of(ci * chunk, chunk), chunk)
        l0, l1, l2 = lse_s[0, rows, :], lse_s[1, rows, :], lse_s[2, rows, :]
        mx = jnp.maximum(jnp.maximum(l0, l1), l2)
        e0, e1, e2 = jnp.exp(l0 - mx), jnp.exp(l1 - mx), jnp.exp(l2 - mx)
        num = e0 * out_s[0, rows, :] + e1 * out_s[1, rows, :] + e2 * out_s[2, rows, :]
        o_ref[0, rows, :] = (num / (e0 + e1 + e2)).astype(o_ref.dtype)
        return carry

    lax.fori_loop(0, SEQ // chunk, combine, 0)


def _attention(qkv, chunk=256, chains=16):
    b, s, _ = qkv.shape
    assert s == SEQ

    def head_block(part, group):
        col0 = part * ATTN_HEADS + group * HEADS_PER_GROUP
        return pl.BlockSpec((1, s, HEAD_DIM), lambda bi, h: (bi, 0, col0 + h))

    in_specs = [head_block(part, group) for group in range(N_GROUPS) for part in range(3)]
    return pl.pallas_call(
        functools.partial(_attn_kernel, chunk=chunk, chains=chains),
        grid=(b, HEADS_PER_GROUP),
        in_specs=in_specs,
        out_specs=pl.BlockSpec((1, s, HEAD_DIM), lambda bi, h: (bi, 0, h)),
        out_shape=jax.ShapeDtypeStruct((b, s, GROUP_WIDTH), jnp.bfloat16),
        scratch_shapes=[pltpu.VMEM((s, HEAD_DIM), jnp.float32),
                        pltpu.VMEM((s, HEAD_DIM), jnp.float32),
                        pltpu.VMEM((s, HEAD_DIM), jnp.float32),
                        pltpu.VMEM((N_GROUPS, s, HEAD_DIM), jnp.float32),
                        pltpu.VMEM((N_GROUPS, s, HEAD_DIM), jnp.float32)],
        compiler_params=_params(("arbitrary", "arbitrary")),
        name="dilated_attn",
    )(*([qkv] * 9))


def _resid_layer_norm(x, f, g, b):
    z = DEEPNORM_ALPHA * x + f
    mu = jnp.mean(z, axis=-1, keepdims=True)
    zc = z - mu
    var = jnp.mean(zc * zc, axis=-1, keepdims=True)
    return zc * lax.rsqrt(var + LN_EPS) * g + b


def _pack_bf16_pairs(y):
    c = y.shape[1] // 2
    bits = pltpu.bitcast(y.astype(jnp.bfloat16).astype(jnp.float32), jnp.uint32)
    return bits[:, :c] | (bits[:, c:] >> 16)


def _unpack_bf16_pairs(w):
    hi = pltpu.bitcast(w & jnp.uint32(0xFFFF0000), jnp.float32).astype(jnp.bfloat16)
    lo = pltpu.bitcast(w << 16, jnp.float32).astype(jnp.bfloat16)
    return hi, lo


def _mixer_tail_kernel(cb_ref, cc_ref, ch_ref, cch_ref, chh_ref, pu_ref, puh_ref, attn_ref,
                       g0a_ref, g0b_ref, g1a_ref, g1b_ref, g2a_ref, g2b_ref, x_ref,
                       convw_ref, pscale_ref, lng_ref, lnb_ref,
                       wconv_ref, wattn_ref, wpool_ref, pmix_ref, wo_ref, o_ref, obf_ref,
                       *, tm, tiles_per_seq, packed):
    seq_tile = pl.program_id(0) % tiles_per_seq
    not_first = (seq_tile != 0).astype(jnp.float32)
    f32 = lambda ref: ref[...].astype(jnp.float32)

    u = f32(cc_ref) * f32(ch_ref)
    uh = f32(cch_ref) * f32(chh_ref) * not_first
    ext = jnp.concatenate([uh, u], axis=0)
    u1 = pltpu.roll(ext, 1, 0)[CONV_HALO:]
    u2 = pltpu.roll(ext, 2, 0)[CONV_HALO:]
    cw = convw_ref[...]
    conv = u2 * cw[0:1] + u1 * cw[1:2]
    conv = conv + u * cw[2:3]
    ya_in = (f32(cb_ref) * conv).astype(jnp.bfloat16)

    pu = f32(pu_ref)
    pext = jnp.concatenate([f32(puh_ref) * not_first, pu], axis=0)
    pos = seq_tile * tm + lax.broadcasted_iota(jnp.int32, (tm, 1), 0) + 1
    pscale = pscale_ref[...]
    mixed = []
    for gi, w in enumerate(POOL_WINDOWS):
        cols = slice(gi * POOL_GROUP_DIM, (gi + 1) * POOL_GROUP_DIM)
        run = pext[:, cols]
        step = 1
        while step < w:
            run = run + pltpu.roll(run, step, 0)
            step *= 2
        inv_cnt = 1.0 / jnp.minimum(pos, w).astype(jnp.float32)
        pooled = run[POOL_HALO:] * inv_cnt - pu[:, cols]
        mix = jnp.dot(pooled.astype(jnp.bfloat16), pmix_ref[gi], preferred_element_type=jnp.float32)
        mixed.append((mix * pscale[:, cols]).astype(jnp.bfloat16))
    yc_in = jnp.concatenate(mixed, axis=1)

    pieces = []
    for hc, (ga, gb, gc) in enumerate(((g0a_ref, g1a_ref, g2a_ref), (g0b_ref, g1b_ref, g2b_ref))):
        for q in range(CONV_WIDTH // MERGE_COLS):
            gcols = slice(q * MERGE_COLS, (q + 1) * MERGE_COLS)
            cols = slice(hc * CONV_WIDTH + q * MERGE_COLS, hc * CONV_WIDTH + (q + 1) * MERGE_COLS)
            sig = lambda ref: jax.nn.sigmoid(ref[:, gcols].astype(jnp.float32))
            part = sig(ga) * jnp.dot(ya_in, wconv_ref[:, cols], preferred_element_type=jnp.float32)
            part = part + sig(gb) * jnp.dot(attn_ref[...], wattn_ref[:, cols],
                                            preferred_element_type=jnp.float32)
            part = part + sig(gc) * jnp.dot(yc_in, wpool_ref[:, cols], preferred_element_type=jnp.float32)
            pieces.append(part.astype(jnp.bfloat16))
    merged = jnp.concatenate(pieces, axis=1)

    f = jnp.dot(merged, wo_ref[...], preferred_element_type=jnp.float32)
    y = _resid_layer_norm(x_ref[...], f, lng_ref[...], lnb_ref[...])
    o_ref[...] = y
    obf_ref[...] = _pack_bf16_pairs(y) if packed else y.astype(jnp.bfloat16)


def _mixer_tail(conv_p, pool_gates_p, attn, x, conv_w, pool_scale, ln_g, ln_b,
                w_conv_bf, w_attn_bf, w_pool_bf, pool_mix_bf, w_o_bf, packed=False, tm=256):
    m = conv_p.shape[0]
    gate_blocks = N_GROUPS * D_MODEL // CONV_WIDTH
    tiles_per_seq = SEQ // tm
    row = lambda width, col: pl.BlockSpec((tm, width), lambda i: (i, col))

    def halo(rows, col):
        per_tile = tm // rows
        return pl.BlockSpec((rows, CONV_WIDTH), lambda i: (jnp.maximum(i * per_tile - 1, 0), col))

    full = lambda shape: pl.BlockSpec(shape, lambda i: (0,) * len(shape), pipeline_mode=pl.Buffered(1))
    in_specs = [row(CONV_WIDTH, 0), row(CONV_WIDTH, 1), row(CONV_WIDTH, 2),
                halo(CONV_HALO, 1), halo(CONV_HALO, 2),
                row(POOL_WIDTH, 0), halo(POOL_HALO, 0),
                row(GROUP_WIDTH, 0)]
    in_specs += [row(CONV_WIDTH, 1 + gb) for gb in range(gate_blocks)]
    in_specs += [row(D_MODEL, 0),
                 full((CONV_KERNEL, CONV_WIDTH)), full((1, POOL_WIDTH)), full((1, D_MODEL)), full((1, D_MODEL)),
                 full((CONV_WIDTH, D_MODEL)), full((GROUP_WIDTH, D_MODEL)), full((POOL_WIDTH, D_MODEL)),
                 full((len(POOL_WINDOWS), POOL_GROUP_DIM, POOL_GROUP_DIM)), full((D_MODEL, D_MODEL))]
    return pl.pallas_call(
        functools.partial(_mixer_tail_kernel, tm=tm, tiles_per_seq=tiles_per_seq, packed=packed),
        grid=(m // tm,),
        in_specs=in_specs,
        out_specs=[row(D_MODEL, 0), row(D_MODEL // 2 if packed else D_MODEL, 0)],
        out_shape=[jax.ShapeDtypeStruct((m, D_MODEL), jnp.float32),
                   jax.ShapeDtypeStruct((m, D_MODEL // 2), jnp.uint32) if packed
                   else jax.ShapeDtypeStruct((m, D_MODEL), jnp.bfloat16)],
        compiler_params=_params(("arbitrary",)),
        name="mixer_tail",
    )(conv_p, conv_p, conv_p, conv_p, conv_p, pool_gates_p, pool_gates_p, attn,
      *([pool_gates_p] * gate_blocks), x,
      conv_w, pool_scale.reshape(1, POOL_WIDTH), ln_g.reshape(1, D_MODEL), ln_b.reshape(1, D_MODEL),
      w_conv_bf, w_attn_bf, w_pool_bf, pool_mix_bf, w_o_bf)


def _route_tile(y, w_ref, idx_ref, gate_ref, rank_ref, cnt_ref, carry_ref, *, tm):
    @pl.when(pl.program_id(0) == 0)
    def _():
        carry_ref[...] = jnp.zeros_like(carry_ref)

    w = w_ref[...]
    y_hi, w_hi = y.astype(jnp.bfloat16), w.astype(jnp.bfloat16)
    y_lo = (y - y_hi.astype(jnp.float32)).astype(jnp.bfloat16)
    w_lo = (w - w_hi.astype(jnp.float32)).astype(jnp.bfloat16)
    logits = (jnp.dot(y_hi, w_hi, preferred_element_type=jnp.float32)
              + jnp.dot(y_lo, w_hi, preferred_element_type=jnp.float32)
              + jnp.dot(y_hi, w_lo, preferred_element_type=jnp.float32))
    lane = lax.broadcasted_iota(jnp.int32, logits.shape, 1)
    logits = jnp.where(lane < N_EXPERTS, logits, -jnp.inf)
    m1 = jnp.max(logits, axis=-1, keepdims=True)
    i1 = jnp.min(jnp.where(logits == m1, lane, LANES), axis=-1, keepdims=True)
    rest = jnp.where(lane == i1, -jnp.inf, logits)
    m2 = jnp.max(rest, axis=-1, keepdims=True)
    i2 = jnp.min(jnp.where(rest == m2, lane, LANES), axis=-1, keepdims=True)
    e = jnp.exp(m2 - m1)
    den = 1.0 + e
    idx_ref[:, 0:1] = i1
    idx_ref[:, 1:2] = i2
    gate_ref[:, 0:1] = 1.0 / den
    gate_ref[:, 1:2] = e / den

    oh1 = (lane == i1).astype(jnp.float32)
    oh2 = (lane == i2).astype(jnp.float32)
    both = oh1 + oh2
    r = lax.broadcasted_iota(jnp.int32, (tm, tm), 0)
    c = lax.broadcasted_iota(jnp.int32, (tm, tm), 1)
    strict_lower = (r > c).astype(jnp.bfloat16)
    before = jnp.dot(strict_lower, both.astype(jnp.bfloat16), preferred_element_type=jnp.float32)
    before = before + carry_ref[...]
    rank_ref[:, 0:1] = jnp.sum(oh1 * before, axis=-1, keepdims=True).astype(jnp.int32)
    rank_ref[:, 1:2] = jnp.sum(oh2 * before, axis=-1, keepdims=True).astype(jnp.int32)
    carry_ref[...] += jnp.sum(both, axis=0, keepdims=True)
    cnt_ref[...] = carry_ref[...]


def _route_kernel(x_ref, w_ref, idx_ref, gate_ref, rank_ref, cnt_ref, carry_ref, *, tm):
    _route_tile(x_ref[...], w_ref, idx_ref, gate_ref, rank_ref, cnt_ref, carry_ref, tm=tm)


def _route(x, w_router, tm=512):
    m = x.shape[0]
    w_pad = jnp.pad(w_router, ((0, 0), (0, LANES - N_EXPERTS)))
    pair = pl.BlockSpec((tm, TOP_K), lambda i: (i, 0))
    return pl.pallas_call(
        functools.partial(_route_kernel, tm=tm),
        grid=(m // tm,),
        in_specs=[pl.BlockSpec((tm, D_MODEL), lambda i: (i, 0)),
                  pl.BlockSpec((D_MODEL, LANES), lambda i: (0, 0))],
        out_specs=[pair, pair, pair, pl.BlockSpec((1, LANES), lambda i: (0, 0))],
        out_shape=[jax.ShapeDtypeStruct((m, TOP_K), jnp.int32),
                   jax.ShapeDtypeStruct((m, TOP_K), jnp.float32),
                   jax.ShapeDtypeStruct((m, TOP_K), jnp.int32),
                   jax.ShapeDtypeStruct((1, LANES), jnp.float32)],
        scratch_shapes=[pltpu.VMEM((1, LANES), jnp.float32)],
        compiler_params=_params(("arbitrary",)),
        name="router",
    )(x, w_pad)


def _swiglu_kernel(blk_exp_ref, blk_rows_ref, n_used_ref, x_ref, wg_ref, wu_ref, wd_ref, *refs,
                   tm, sub, last_j, resid_ln):
    del blk_exp_ref, n_used_ref
    i, j = pl.program_id(0), pl.program_id(1)
    if resid_ln:
        res_hbm, g_ref, b_ref, o_ref, obf_ref, res_ref, res_sem = refs
        res_copy = pltpu.make_async_copy(res_hbm.at[pl.ds(pl.multiple_of(i * tm, tm), tm)], res_ref, res_sem)
        xbf_ref = x_ref
    else:
        o_ref, xbf_ref = refs
    n_sub = (blk_rows_ref[i] + sub - 1) // sub

    @pl.when(j == 0)
    def _():
        if resid_ln:
            res_copy.start()
        else:
            half = x_ref.shape[1]
            xbf_ref[:, :half], xbf_ref[:, half:] = _unpack_bf16_pairs(x_ref[...])
        o_ref[...] = jnp.zeros_like(o_ref)

    for v in range(1, tm // sub + 1):
        rows = v * sub

        @pl.when(n_sub == v)
        def _(rows=rows):
            x = xbf_ref[0:rows, :]
            gate = jnp.dot(x, wg_ref[0].astype(jnp.bfloat16), preferred_element_type=jnp.float32)
            up = jnp.dot(x, wu_ref[0].astype(jnp.bfloat16), preferred_element_type=jnp.float32)
            h = (gate * jax.nn.sigmoid(gate) * up).astype(jnp.bfloat16)
            o_ref[0:rows, :] += jnp.dot(h, wd_ref[0].astype(jnp.bfloat16),
                                        preferred_element_type=jnp.float32)

    if resid_ln:
        @pl.when(j == last_j)
        def _():
            res_copy.wait()

            def norm_rows(ci, carry):
                rows = pl.ds(pl.multiple_of(ci * sub, sub), sub)
                y = _resid_layer_norm(res_ref[rows, :], o_ref[rows, :], g_ref[...], b_ref[...])
                o_ref[rows, :] = y
                obf_ref[rows, :] = y.astype(jnp.bfloat16)
                return carry

            lax.fori_loop(0, tm // sub, norm_rows, 0)


def _swiglu_blocks(x, w_gate, w_up, w_down, blk_exp, blk_rows, n_used, tm, tf, sub=256, resid_ln=None):
    r = x.shape[0]
    nf = D_FF // tf
    last = nf - 1

    def live(i, nu):
        return jnp.maximum(jnp.minimum(i, nu[0] - 1), 0)

    def jj(i, j, nu):
        return jnp.where(i < nu[0], j, last)

    once = pl.Buffered(1)
    x_spec = pl.BlockSpec((tm, x.shape[1]), lambda i, j, be, br, nu: (live(i, nu), 0),
                          pipeline_mode=once if resid_ln is not None else None)
    row_out = pl.BlockSpec((tm, D_MODEL), lambda i, j, be, br, nu: (i, 0), pipeline_mode=once)
    vec = pl.BlockSpec((1, D_MODEL), lambda i, j, be, br, nu: (0, 0))
    in_specs = [x_spec,
                pl.BlockSpec((1, D_MODEL, tf), lambda i, j, be, br, nu: (be[live(i, nu)], 0, jj(i, j, nu))),
                pl.BlockSpec((1, D_MODEL, tf), lambda i, j, be, br, nu: (be[live(i, nu)], 0, jj(i, j, nu))),
                pl.BlockSpec((1, tf, D_MODEL), lambda i, j, be, br, nu: (be[live(i, nu)], jj(i, j, nu), 0))]
    args = [x, w_gate, w_up, w_down]
    out_specs = row_out
    out_shape = jax.ShapeDtypeStruct((r, D_MODEL), jnp.float32)
    if resid_ln is not None:
        assert x.dtype == jnp.bfloat16
        res, g, b = resid_ln
        in_specs += [pl.BlockSpec(memory_space=pl.ANY), vec, vec]
        args += [res, g.reshape(1, D_MODEL), b.reshape(1, D_MODEL)]
        out_specs = [row_out, row_out]
        out_shape = [out_shape, jax.ShapeDtypeStruct((r, D_MODEL), jnp.bfloat16)]
        scratch = [pltpu.VMEM((tm, D_MODEL), jnp.float32), pltpu.SemaphoreType.DMA(())]
    else:
        assert x.dtype == jnp.uint32
        scratch = [pltpu.VMEM((tm, D_MODEL), jnp.bfloat16)]
    grid_spec = pltpu.PrefetchScalarGridSpec(
        num_scalar_prefetch=3, grid=(r // tm, nf),
        in_specs=in_specs, out_specs=out_specs, scratch_shapes=scratch)
    return pl.pallas_call(
        functools.partial(_swiglu_kernel, tm=tm, sub=sub, last_j=last, resid_ln=resid_ln is not None),
        grid_spec=grid_spec,
        out_shape=out_shape,
        compiler_params=_params(("arbitrary", "arbitrary")),
        name="swiglu_blocks",
    )(blk_exp, blk_rows, n_used, *args)


ZERO_ROWS = 256


def _dispatch_kernel(zero_slab_ref, dest_ref, x_ref, xs_ref, zero_ref, sem, *, tm):
    @pl.when(pl.program_id(0) == 0)
    def _():
        zero_ref[...] = jnp.zeros_like(zero_ref)
        slabs = [pltpu.make_async_copy(zero_ref, xs_ref.at[pl.ds(z * ZERO_ROWS, ZERO_ROWS)], sem)
                 for z in range(zero_slab_ref.shape[0])]
        for z, slab in enumerate(slabs):
            pl.when(zero_slab_ref[z] != 0)(slab.start)
        for z, slab in enumerate(slabs):
            pl.when(zero_slab_ref[z] != 0)(slab.wait)

    def issue(t, carry):
        for k in range(TOP_K):
            pltpu.make_async_copy(x_ref.at[pl.ds(t, 1)],
                                  xs_ref.at[pl.ds(dest_ref[TOP_K * t + k], 1)], sem).start(priority=k % 2)
        return carry

    lax.fori_loop(0, tm, issue, 0, unroll=8)
    for k in range(TOP_K):
        pltpu.make_async_copy(x_ref, xs_ref.at[pl.ds(0, tm)], sem).wait()


def _dispatch(x, dest_flat, zero_slab, rows, tm=512):
    m, width = x.shape
    grid_spec = pltpu.PrefetchScalarGridSpec(
        num_scalar_prefetch=1, grid=(m // tm,),
        in_specs=[pl.BlockSpec((TOP_K * tm,), lambda i, zs: (i,), memory_space=pltpu.SMEM),
                  pl.BlockSpec((tm, width), lambda i, zs: (i, 0))],
        out_specs=pl.BlockSpec(memory_space=pl.ANY),
        scratch_shapes=[pltpu.VMEM((ZERO_ROWS, width), x.dtype), pltpu.SemaphoreType.DMA(())])
    return pl.pallas_call(
        functools.partial(_dispatch_kernel, tm=tm),
        grid_spec=grid_spec,
        out_shape=jax.ShapeDtypeStruct((rows, width), x.dtype),
        compiler_params=_params(("arbitrary",)),
        name="dispatch",
    )(zero_slab, dest_flat, x)


def _combine_ln_kernel(dest_ref, dest_next_ref, x_ref, gate_ref, g_ref, b_ref, y_ref, o_ref,
                       ybuf_ref, sems, *, tm):
    i, n = pl.program_id(0), pl.num_programs(0)
    slot = lax.rem(i, 2)

    def gather(d_ref, s):
        def issue(t, carry):
            for k in range(TOP_K):
                pltpu.make_async_copy(y_ref.at[pl.ds(d_ref[TOP_K * t + k], 1)],
                                      ybuf_ref.at[s, k, pl.ds(t, 1)], sems.at[s]).start(priority=k % 2)
            return carry

        lax.fori_loop(0, tm, issue, 0, unroll=8)

    @pl.when(i == 0)
    def _():
        gather(dest_ref, 0)

    @pl.when(i + 1 < n)
    def _():
        gather(dest_next_ref, 1 - slot)

    for k in range(TOP_K):
        pltpu.make_async_copy(y_ref.at[pl.ds(0, tm)], ybuf_ref.at[slot, k], sems.at[slot]).wait()
    gate = gate_ref[...]
    f = ybuf_ref[slot, 0] * gate[:, 0:1] + ybuf_ref[slot, 1] * gate[:, 1:2]
    o_ref[...] = _resid_layer_norm(x_ref[...], f, g_ref[...], b_ref[...])


def _combine_ln(x, y, dest_flat, gates, g, b, tm=256):
    m = x.shape[0]
    n_tiles = m // tm
    row = pl.BlockSpec((tm, D_MODEL), lambda i: (i, 0))
    vec = pl.BlockSpec((1, D_MODEL), lambda i: (0, 0))
    return pl.pallas_call(
        functools.partial(_combine_ln_kernel, tm=tm),
        grid=(n_tiles,),
        in_specs=[pl.BlockSpec((TOP_K * tm,), lambda i: (i,), memory_space=pltpu.SMEM),
                  pl.BlockSpec((TOP_K * tm,), lambda i: (jnp.minimum(i + 1, n_tiles - 1),),
                               memory_space=pltpu.SMEM),
                  row, pl.BlockSpec((tm, TOP_K), lambda i: (i, 0)), vec, vec,
                  pl.BlockSpec(memory_space=pl.ANY)],
        out_specs=row,
        out_shape=jax.ShapeDtypeStruct((m, D_MODEL), jnp.float32),
        scratch_shapes=[pltpu.VMEM((2, TOP_K, tm, D_MODEL), jnp.float32),
                        pltpu.SemaphoreType.DMA((2,))],
        compiler_params=_params(("arbitrary",)),
        name="combine_ln",
    )(dest_flat, dest_flat, x, gates, g.reshape(1, D_MODEL), b.reshape(1, D_MODEL), y)


def _rope_tables(seq):
    inv_freq = ROPE_THETA ** (-jnp.arange(0, ROPE_DIM, 2, dtype=jnp.float32) / ROPE_DIM)
    ang = jnp.arange(seq, dtype=jnp.float32)[:, None] * inv_freq[None, :]
    cos, sin = jnp.cos(ang), jnp.sin(ang)
    pad = HEAD_DIM - ROPE_DIM
    cos_full = jnp.concatenate([cos, cos, jnp.ones((seq, pad), jnp.float32)], axis=1)
    sin_full = jnp.concatenate([-sin, sin, jnp.zeros((seq, pad), jnp.float32)], axis=1)
    return cos_full, sin_full


def _mixer(x, x_bf, batch, w_in, conv_w, w_br_conv, w_br_attn, pool_mix, pool_scale, w_br_pool, w_o,
           ln_g, ln_b, rope_cos, rope_sin, packed=False):
    conv_cols = CONV_KERNEL * CONV_WIDTH
    pool_col = conv_cols + QKV_WIDTH
    conv_p = _project(x_bf, w_in, 0, conv_cols, 1536, jnp.bfloat16)
    qkv = _project(x_bf, w_in, conv_cols, QKV_WIDTH, QKV_WIDTH // 3, jnp.bfloat16,
                   rope=(rope_cos, rope_sin), rope_tiles=2)
    pool_gates_p = _project(x_bf, w_in, pool_col, POOL_WIDTH + N_GROUPS * D_MODEL, 1024, jnp.bfloat16)
    attn = _attention(qkv.reshape(batch, SEQ, QKV_WIDTH))
    attn = attn.reshape(batch * SEQ, GROUP_WIDTH)
    bf = lambda w: w.astype(jnp.bfloat16)
    return _mixer_tail(conv_p, pool_gates_p, attn, x, conv_w, pool_scale, ln_g, ln_b,
                       bf(w_br_conv), bf(w_br_attn), bf(w_br_pool), bf(pool_mix), bf(w_o), packed=packed)


def _dense_ffn(x, x_bf, w_gate, w_up, w_down, ln_g, ln_b, tm=1024, tf=512):
    n_blk = x.shape[0] // tm
    blk_exp = jnp.zeros((n_blk,), jnp.int32)
    blk_rows = jnp.full((n_blk,), tm, jnp.int32)
    n_used = jnp.full((1,), n_blk, jnp.int32)
    return _swiglu_blocks(x_bf, w_gate[None], w_up[None], w_down[None], blk_exp, blk_rows, n_used, tm, tf,
                          resid_ln=(x, ln_g, ln_b))


def _moe_ffn(x, x_packed, w_router, w_gate, w_up, w_down, ln_g, ln_b, tm=1024, tf=512):
    m = x.shape[0]
    idx, gates, rank, cnt = _route(x, w_router)
    counts = cnt[0, :N_EXPERTS].astype(jnp.int32)
    padded = (counts + tm - 1) // tm * tm
    pend = jnp.cumsum(padded)
    pstart = pend - padded
    dest_flat = (pstart[idx] + rank).reshape(-1)
    n_blk = m * TOP_K // tm + N_EXPERTS
    blk_start = jnp.arange(n_blk, dtype=jnp.int32) * tm
    blk_exp = jnp.minimum(jnp.sum(blk_start[:, None] >= pend[None, :], axis=1), N_EXPERTS - 1).astype(jnp.int32)
    blk_rows = jnp.clip(pstart[blk_exp] + counts[blk_exp] - blk_start, 0, tm).astype(jnp.int32)
    n_used = (pend[-1:] // tm).astype(jnp.int32)
    blk_rows = jnp.where(jnp.arange(n_blk) < n_used[0], blk_rows, 0)
    slab_end = (jnp.arange(n_blk * tm // ZERO_ROWS, dtype=jnp.int32) + 1) * ZERO_ROWS
    slab_exp = jnp.repeat(blk_exp, tm // ZERO_ROWS)
    zero_slab = (slab_end > pstart[slab_exp] + counts[slab_exp]).astype(jnp.int32)
    xs = _dispatch(x_packed, dest_flat, zero_slab, n_blk * tm)
    ys = _swiglu_blocks(xs, w_gate, w_up, w_down, blk_exp, blk_rows, n_used, tm, tf)
    return _combine_ln(x, ys, dest_flat, gates, ln_g, ln_b)


def kernel(x, l0_w_in, l0_conv_w, l0_w_br_conv, l0_w_br_attn, l0_pool_mix, l0_pool_scale, l0_w_br_pool, l0_w_o, l0_ln1_g, l0_ln1_b, l0_ffn_gate, l0_ffn_up, l0_ffn_down, l0_ln2_g, l0_ln2_b, l1_w_in, l1_conv_w, l1_w_br_conv, l1_w_br_attn, l1_pool_mix, l1_pool_scale, l1_w_br_pool, l1_w_o, l1_ln1_g, l1_ln1_b, l1_router, l1_exp_gate, l1_exp_up, l1_exp_down, l1_ln2_g, l1_ln2_b):
    batch, seq, d = x.shape
    assert seq == SEQ and d == D_MODEL
    rope_cos, rope_sin = _rope_tables(seq)
    x0 = x.reshape(batch * seq, d)
    x1, x1_bf = _mixer(x0, x0.astype(jnp.bfloat16), batch, l0_w_in, l0_conv_w, l0_w_br_conv, l0_w_br_attn,
                       l0_pool_mix, l0_pool_scale, l0_w_br_pool, l0_w_o, l0_ln1_g, l0_ln1_b, rope_cos, rope_sin)
    x2, x2_bf = _dense_ffn(x1, x1_bf, l0_ffn_gate, l0_ffn_up, l0_ffn_down, l0_ln2_g, l0_ln2_b)
    x3, x3_packed = _mixer(x2, x2_bf, batch, l1_w_in, l1_conv_w, l1_w_br_conv, l1_w_br_attn,
                           l1_pool_mix, l1_pool_scale, l1_w_br_pool, l1_w_o, l1_ln1_g, l1_ln1_b,
                           rope_cos, rope_sin, packed=True)
    x4 = _moe_ffn(x3, x3_packed, l1_router, l1_exp_gate, l1_exp_up, l1_exp_down, l1_ln2_g, l1_ln2_b)
    return x4.reshape(batch, seq, d)
```
